```python
import jax, jax.numpy as jnp
from jax import lax
import numpy as np

D_MODEL = 1024
BATCH = 4
SEQ = 4096
DEPTH = 2
DEC_BATCH = 32
DEC_SEQ = 1
PAST_LEN = 16384
PAGE_SIZE = 128

HEAD_DIM = 64
N_BRANCH = 4
BRANCH_W = D_MODEL // 2
RWKV_HEADS = BRANCH_W // HEAD_DIM
RWKV_DECAY_RANK = 64
RWKV_ICL_RANK = 64
RWKV_GATE_RANK = 128
RWKV_IN = 3 * BRANCH_W + RWKV_DECAY_RANK + RWKV_ICL_RANK + RWKV_GATE_RANK
RWKV_SPLITS = [BRANCH_W, 2 * BRANCH_W, 3 * BRANCH_W, 3 * BRANCH_W + RWKV_DECAY_RANK,
               3 * BRANCH_W + RWKV_DECAY_RANK + RWKV_ICL_RANK]
RWKV_GN_EPS = 64e-5
S5_GROUP = 16
S5_GROUPS = BRANCH_W // S5_GROUP
S5_STATE = 64
S5_DT_MIN = 1e-3
S5_DT_MAX = 1e-1
MOBA_HEADS = BRANCH_W // HEAD_DIM
MOBA_BLOCK = 256
MOBA_TOPK = 3
MOBA_Q_CHUNK = 32
SB_HEADS = BRANCH_W // HEAD_DIM
SB_Q_BLOCK = 128
ROT_DIM = HEAD_DIM // 4
ROPE_THETA = 500000.0
D_FF = -(-8 * D_MODEL // (3 * 256)) * 256
IN_SPLITS = [N_BRANCH * D_MODEL, N_BRANCH * D_MODEL + RWKV_IN,
             N_BRANCH * D_MODEL + RWKV_IN + BRANCH_W, N_BRANCH * D_MODEL + RWKV_IN + 4 * BRANCH_W]
IN_COLS = N_BRANCH * D_MODEL + RWKV_IN + 7 * BRANCH_W

kernel_name = 'hybrid_rwkv7_s5_moba_stickbreak_decode_step'


def _rmsnorm(x, g, eps=1e-6):
    xf = x.astype(jnp.float32)
    y = xf * lax.rsqrt(jnp.mean(xf * xf, axis=-1, keepdims=True) + eps)
    return (y * g.astype(jnp.float32)).astype(x.dtype)


def _rope_partial(x, pos):
    f32 = jnp.float32
    half = ROT_DIM // 2
    inv = ROPE_THETA ** (-jnp.arange(half, dtype=f32) * 2.0 / ROT_DIM)
    ang = pos.astype(f32)[:, None] * inv[None, :]
    cos = jnp.cos(ang)[None, :, None, :]
    sin = jnp.sin(ang)[None, :, None, :]
    xf = x.astype(f32)
    x1, x2, rest = xf[..., :half], xf[..., half:ROT_DIM], xf[..., ROT_DIM:]
    out = jnp.concatenate([x1 * cos - x2 * sin, x2 * cos + x1 * sin, rest], axis=-1)
    return out.astype(x.dtype)


def _pad_to_block(t, blk):
    pad = (-t.shape[1]) % blk
    return jnp.pad(t, ((0, 0), (0, pad), (0, 0), (0, 0)))


def _sweep(fn, q, q_pos, cap):
    n = q.shape[1]
    blk = min(cap, n)
    nb = -(-n // blk)
    pad = nb * blk - n
    qp = jnp.pad(q, ((0, 0), (0, pad), (0, 0), (0, 0)))
    pp = jnp.pad(q_pos, (0, pad), mode='edge')
    qb = jnp.swapaxes(qp.reshape(q.shape[0], nb, blk, q.shape[2], q.shape[3]), 0, 1)
    pb = pp.reshape(nb, blk)
    out = lax.map(lambda a: fn(a[0], a[1]), (qb, pb))
    out = jnp.swapaxes(out, 0, 1).reshape(q.shape[0], nb * blk, out.shape[3], out.shape[4])
    return out[:, :n]


def _moba_block(qb, pb, k_blocks, v_blocks, k_mean, topk):
    f32 = jnp.float32
    B, Q, H, Dh = qb.shape
    nb = k_blocks.shape[1]
    own = pb // MOBA_BLOCK
    qf = qb.astype(f32)
    gate = jnp.einsum('bqhd,bnhd->bqhn', qf, k_mean)
    fully_past = jnp.arange(nb)[None, :] < own[:, None]
    gate = jnp.where(fully_past[None, :, None, :], gate, -jnp.inf)
    _, sel = lax.top_k(gate, topk)
    sel_ok = sel < own[None, :, None, None]
    own_b = jnp.broadcast_to(own[None, :, None, None], (B, Q, H, 1)).astype(sel.dtype)
    idx = jnp.concatenate([sel, own_b], axis=-1)
    ok = jnp.concatenate([sel_ok, jnp.ones((B, Q, H, 1), dtype=bool)], axis=-1)
    bi = jnp.arange(B)[:, None, None, None]
    hi = jnp.arange(H)[None, None, :, None]
    kg = k_blocks[bi, idx, :, hi, :].astype(f32)
    vg = v_blocks[bi, idx, :, hi, :].astype(f32)
    s = jnp.einsum('bqhd,bqhnkd->bqhnk', qf, kg) * (Dh ** -0.5)
    key_pos = idx[..., None] * MOBA_BLOCK + jnp.arange(MOBA_BLOCK)
    valid = ok[..., None] & (key_pos <= pb[None, :, None, None, None])
    s = jnp.where(valid, s, -jnp.inf).reshape(B, Q, H, -1)
    w = jax.nn.softmax(s, axis=-1).reshape(B, Q, H, topk + 1, MOBA_BLOCK)
    return jnp.einsum('bqhnk,bqhnkd->bqhd', w, vg)


def _moba(q, pos, k_all, v_all):
    k_full = _pad_to_block(k_all, MOBA_BLOCK)
    v_full = _pad_to_block(v_all, MOBA_BLOCK)
    B, T = k_full.shape[0], k_full.shape[1]
    nb = T // MOBA_BLOCK
    k_blocks = k_full.reshape(B, nb, MOBA_BLOCK, MOBA_HEADS, HEAD_DIM)
    v_blocks = v_full.reshape(B, nb, MOBA_BLOCK, MOBA_HEADS, HEAD_DIM)
    k_mean = jnp.mean(k_blocks.astype(jnp.float32), axis=2)
    topk = min(MOBA_TOPK, nb)
    return _sweep(lambda qb, pb: _moba_block(qb, pb, k_blocks, v_blocks, k_mean, topk),
                  q, pos, MOBA_Q_CHUNK)


def _sb_block(qb, pb, k, v, k_pos):
    f32 = jnp.float32
    z = jnp.einsum('bqhd,bkhd->bhqk', qb.astype(f32), k.astype(f32)) * (HEAD_DIM ** -0.5)
    mask = (k_pos[None, :] < pb[:, None])[None, None]
    lneg = jnp.where(mask, jax.nn.log_sigmoid(-z), 0.0)
    between = lax.cumsum(lneg, axis=3, reverse=True) - lneg
    w = jnp.where(mask, jnp.exp(jax.nn.log_sigmoid(z) + between), 0.0)
    return jnp.einsum('bhqk,bkhd->bqhd', w, v.astype(f32))


def _stick_breaking(q, pos, k_all, v_all):
    k_pos = jnp.arange(k_all.shape[1], dtype=jnp.int32)
    return _sweep(lambda qb, pb: _sb_block(qb, pb, k_all, v_all, k_pos), q, pos, SB_Q_BLOCK)


def _rwkv(pa, shift0, s0, p):
    f32 = jnp.float32
    B, L, _ = pa.shape
    H, Dh = RWKV_HEADS, HEAD_DIM
    prev = jnp.concatenate([shift0[:, None, :].astype(pa.dtype), pa[:, :-1]], axis=1)
    xs = pa + (prev - pa) * p['rwkv_mu']
    r, k, v, wl, al, gl = jnp.split(xs, RWKV_SPLITS, axis=-1)
    w_log = -jax.nn.softplus(-(p['rwkv_w0'] + jnp.tanh(wl) @ p['rwkv_w2'])) - 0.5
    decay = jnp.exp(-jnp.exp(w_log.astype(f32)))
    a = jax.nn.sigmoid(p['rwkv_a0'] + al @ p['rwkv_a2'])
    g = jax.nn.sigmoid(gl) @ p['rwkv_g2']
    heads = lambda t: t.reshape(B, L, H, Dh).astype(f32)
    kk = heads(k * p['rwkv_k_k'])
    kk = kk / jnp.maximum(jnp.linalg.norm(kk, axis=-1, keepdims=True), 1e-12)
    k = k * (1.0 + (a - 1.0) * p['rwkv_k_a'])
    rh, kh, vh, wh, ah = heads(r), heads(k), heads(v), heads(decay), heads(a)

    def step(S, inp):
        r_t, w_t, kk_t, b_t, v_t, k_t = inp
        S = (S * w_t[:, :, None, :]
             - jnp.einsum('bhvk,bhk->bhv', S, kk_t)[..., None] * b_t[:, :, None, :]
             + v_t[..., None] * k_t[:, :, None, :])
        return S, jnp.einsum('bhvk,bhk->bhv', S, r_t)

    seq = (jnp.swapaxes(rh, 0, 1), jnp.swapaxes(wh, 0, 1), jnp.swapaxes(kk, 0, 1),
           jnp.swapaxes(kk * ah, 0, 1), jnp.swapaxes(vh, 0, 1), jnp.swapaxes(kh, 0, 1))
    s_fin, y = lax.scan(step, s0.astype(f32), seq)
    y = jnp.swapaxes(y, 0, 1)
    mu = jnp.mean(y, axis=-1, keepdims=True)
    var = jnp.mean(jnp.square(y - mu), axis=-1, keepdims=True)
    y = ((y - mu) * lax.rsqrt(var + RWKV_GN_EPS)).reshape(B, L, BRANCH_W) * p['rwkv_ln_g'] + p['rwkv_ln_b']
    bonus = jnp.sum(rh * kh * p['rwkv_r_k'].astype(f32), axis=-1, keepdims=True) * vh
    y = (y + bonus.reshape(B, L, BRANCH_W)) * g
    return y, pa[:, -1], s_fin


def _cplx_combine(e1, e2):
    a1r, a1i, b1r, b1i = e1
    a2r, a2i, b2r, b2i = e2
    return (a2r * a1r - a2i * a1i, a2r * a1i + a2i * a1r,
            a2r * b1r - a2i * b1i + b2r, a2r * b1i + a2i * b1r + b2i)


def _s5(u, h0_re, h0_im, p):
    f32 = jnp.float32
    B, L, _ = u.shape
    uf = u.astype(f32)
    ug = uf.reshape(B, L, S5_GROUPS, S5_GROUP)
    a_re = p['s5_a_re'].astype(f32)
    a_im = p['s5_a_im'].astype(f32)
    dt = jnp.exp(p['s5_log_dt'].astype(f32))[:, None]
    mag = jnp.exp(a_re * dt)
    ab_re, ab_im = mag * jnp.cos(a_im * dt), mag * jnp.sin(a_im * dt)
    den = a_re * a_re + a_im * a_im
    n_re = ab_re - 1.0
    cf_re = (n_re * a_re + ab_im * a_im) / den
    cf_im = (ab_im * a_re - n_re * a_im) / den
    b_re = p['s5_b_re'].astype(f32)
    b_im = p['s5_b_im'].astype(f32)
    bb_re = cf_re[..., None] * b_re - cf_im[..., None] * b_im
    bb_im = cf_re[..., None] * b_im + cf_im[..., None] * b_re
    bu_re = jnp.einsum('blgc,gpc->blgp', ug, bb_re)
    bu_im = jnp.einsum('blgc,gpc->blgp', ug, bb_im)
    elems = (jnp.broadcast_to(ab_re, bu_re.shape), jnp.broadcast_to(ab_im, bu_re.shape), bu_re, bu_im)
    cum_re, cum_im, h_re, h_im = lax.associative_scan(_cplx_combine, elems, axis=1)
    r0 = h0_re.astype(f32)[:, None]
    i0 = h0_im.astype(f32)[:, None]
    h_re, h_im = h_re + cum_re * r0 - cum_im * i0, h_im + cum_re * i0 + cum_im * r0
    y = (jnp.einsum('blgp,gcp->blgc', h_re, p['s5_c_re'].astype(f32))
         - jnp.einsum('blgp,gcp->blgc', h_im, p['s5_c_im'].astype(f32)))
    y = y.reshape(B, L, BRANCH_W) + p['s5_d'] * uf
    yg = jax.nn.gelu(y)
    out = yg * jax.nn.sigmoid(yg @ p['s5_w_glu'] + p['s5_b_glu'])
    return out, h_re[:, -1], h_im[:, -1]


def _gather_pages(pool, page_table):
    g = pool[page_table]
    return g.reshape(g.shape[0], g.shape[1] * g.shape[2], g.shape[3], g.shape[4])


def _layer(x, c, pos, shift0, s0, h0_re, h0_im, past_mk, past_mv, past_sk, past_sv, p):
    f32 = jnp.float32
    B, L, _ = x.shape
    mod = jnp.einsum('bd,de->be', jax.nn.silu(c), p['ada_w']) + p['ada_b']
    sh_m, sc_m, g_m, sh_f, sc_f, g_f = jnp.split(mod[:, None, :], 6, axis=-1)
    h = _rmsnorm(x, p['norm_mix_g']) * (1.0 + sc_m) + sh_m
    proj = jnp.einsum('bld,de->ble', h, p['w_in'])
    gates, pa, pb, pc, pd = jnp.split(proj, IN_SPLITS, axis=-1)
    o_a, new_shift, new_s = _rwkv(pa, shift0, s0, p)
    o_b, new_re, new_im = _s5(pb, h0_re, h0_im, p)
    q, k, v = [t.reshape(B, L, MOBA_HEADS, HEAD_DIM) for t in jnp.split(pc, 3, axis=-1)]
    q = _rope_partial(_rmsnorm(q, p['moba_q_norm']), pos)
    k = _rope_partial(_rmsnorm(k, p['moba_k_norm']), pos)
    k_all = k if past_mk is None else jnp.concatenate([past_mk.astype(k.dtype), k], axis=1)
    v_all = v if past_mv is None else jnp.concatenate([past_mv.astype(v.dtype), v], axis=1)
    o_c = _moba(q, pos, k_all, v_all).reshape(B, L, BRANCH_W)
    q2, k2, v2 = [t.reshape(B, L, SB_HEADS, HEAD_DIM) for t in jnp.split(pd, 3, axis=-1)]
    k2_all = k2 if past_sk is None else jnp.concatenate([past_sk.astype(k2.dtype), k2], axis=1)
    v2_all = v2 if past_sv is None else jnp.concatenate([past_sv.astype(v2.dtype), v2], axis=1)
    o_d = _stick_breaking(q2, pos, k2_all, v2_all).reshape(B, L, BRANCH_W)
    outs = jnp.stack([o_a.astype(f32), o_b.astype(f32), o_c.astype(f32), o_d.astype(f32)], axis=2)
    branch = jnp.einsum('blnw,nwd->blnd', outs, p['w_branch'])
    gate = jax.nn.sigmoid(gates.reshape(B, L, N_BRANCH, D_MODEL).astype(f32))
    mixed = jnp.sum(gate * branch, axis=2)
    x = (x + g_m * jnp.einsum('bld,de->ble', mixed, p['w_out'])).astype(x.dtype)
    h2 = _rmsnorm(x, p['norm_ffn_g']) * (1.0 + sc_f) + sh_f
    up_g, up_v = jnp.split(jnp.einsum('bld,df->blf', h2, p['ffn_w_up']), 2, axis=-1)
    x = (x + g_f * jnp.einsum('blf,fd->bld', jax.nn.silu(up_g) * up_v, p['ffn_w_down'])).astype(x.dtype)
    return x, (k, v, k2, v2, new_s, new_shift, new_re, new_im)


def setup_inputs(seed: int = 0) -> dict:
    key = jax.random.key(seed)
    f32 = jnp.float32

    def nrm(i, shape, s=1.0):
        return s * jax.random.normal(jax.random.fold_in(key, i), shape, f32)

    def uni(i, shape, lo, hi):
        return jax.random.uniform(jax.random.fold_in(key, i), shape, f32, lo, hi)

    n_pages = PAST_LEN // PAGE_SIZE
    n_used = DEC_BATCH * n_pages
    n_pool = n_used + n_used // 4
    perm = jax.random.permutation(jax.random.fold_in(key, 12), n_pool)
    page_table = perm[:n_used].reshape(DEC_BATCH, n_pages).astype(jnp.int32)
    a_im = jnp.pi * jnp.arange(S5_STATE, dtype=f32)
    return {
        'x_prompt': nrm(0, (BATCH, SEQ, D_MODEL)),
        'x_sample': nrm(1, (DEC_BATCH, DEC_SEQ, D_MODEL)),
        'c_prompt': nrm(2, (BATCH, D_MODEL)),
        'c_sample': nrm(3, (DEC_BATCH, D_MODEL)),
        'cache_moba_k': nrm(4, (DEPTH, n_pool, PAGE_SIZE, MOBA_HEADS, HEAD_DIM)),
        'cache_moba_v': nrm(5, (DEPTH, n_pool, PAGE_SIZE, MOBA_HEADS, HEAD_DIM)),
        'cache_sb_k': nrm(6, (DEPTH, n_pool, PAGE_SIZE, SB_HEADS, HEAD_DIM)),
        'cache_sb_v': nrm(7, (DEPTH, n_pool, PAGE_SIZE, SB_HEADS, HEAD_DIM)),
        'state_rwkv': nrm(8, (DEPTH, DEC_BATCH, RWKV_HEADS, HEAD_DIM, HEAD_DIM), 0.1),
        'state_rwkv_shift': nrm(9, (DEPTH, DEC_BATCH, RWKV_IN)),
        'state_s5_re': nrm(10, (DEPTH, DEC_BATCH, S5_GROUPS, S5_STATE), 0.1),
        'state_s5_im': nrm(11, (DEPTH, DEC_BATCH, S5_GROUPS, S5_STATE), 0.1),
        'page_table': page_table,
        'norm_mix_g': 1.0 + nrm(13, (DEPTH, D_MODEL), 0.02),
        'norm_ffn_g': 1.0 + nrm(14, (DEPTH, D_MODEL), 0.02),
        'ada_w': nrm(15, (DEPTH, D_MODEL, 6 * D_MODEL), 0.02),
        'ada_b': nrm(16, (DEPTH, 6 * D_MODEL), 0.01),
        'w_in': nrm(17, (DEPTH, D_MODEL, IN_COLS), D_MODEL ** -0.5),
        'rwkv_mu': uni(18, (DEPTH, RWKV_IN), 0.0, 1.0),
        'rwkv_w0': uni(19, (DEPTH, BRANCH_W), -6.0, -1.0),
        'rwkv_w2': nrm(20, (DEPTH, RWKV_DECAY_RANK, BRANCH_W), 0.1 * RWKV_DECAY_RANK ** -0.5),
        'rwkv_a0': nrm(21, (DEPTH, BRANCH_W), 0.1),
        'rwkv_a2': nrm(22, (DEPTH, RWKV_ICL_RANK, BRANCH_W), RWKV_ICL_RANK ** -0.5),
        'rwkv_g2': nrm(23, (DEPTH, RWKV_GATE_RANK, BRANCH_W), RWKV_GATE_RANK ** -0.5),
        'rwkv_k_k': 0.85 + nrm(24, (DEPTH, BRANCH_W), 0.02),
        'rwkv_k_a': 1.0 + nrm(25, (DEPTH, BRANCH_W), 0.02),
        'rwkv_r_k': nrm(26, (DEPTH, RWKV_HEADS, HEAD_DIM), 0.1),
        'rwkv_ln_g': 1.0 + nrm(27, (DEPTH, BRANCH_W), 0.02),
        'rwkv_ln_b': nrm(28, (DEPTH, BRANCH_W), 0.01),
        's5_a_re': -0.5 * jnp.exp(nrm(29, (DEPTH, S5_GROUPS, S5_STATE), 0.02)),
        's5_a_im': a_im + nrm(30, (DEPTH, S5_GROUPS, S5_STATE), 0.01),
        's5_b_re': nrm(31, (DEPTH, S5_GROUPS, S5_STATE, S5_GROUP), (2 * S5_GROUP) ** -0.5),
        's5_b_im': nrm(32, (DEPTH, S5_GROUPS, S5_STATE, S5_GROUP), (2 * S5_GROUP) ** -0.5),
        's5_c_re': nrm(33, (DEPTH, S5_GROUPS, S5_GROUP, S5_STATE), (2 * S5_STATE) ** -0.5),
        's5_c_im': nrm(34, (DEPTH, S5_GROUPS, S5_GROUP, S5_STATE), (2 * S5_STATE) ** -0.5),
        's5_d': nrm(35, (DEPTH, BRANCH_W), 0.5),
        's5_log_dt': uni(36, (DEPTH, S5_GROUPS), float(np.log(S5_DT_MIN)), float(np.log(S5_DT_MAX))),
        's5_w_glu': nrm(37, (DEPTH, BRANCH_W, BRANCH_W), BRANCH_W ** -0.5),
        's5_b_glu': nrm(38, (DEPTH, BRANCH_W), 0.01),
        'moba_q_norm': 1.0 + nrm(39, (DEPTH, HEAD_DIM), 0.02),
        'moba_k_norm': 1.0 + nrm(40, (DEPTH, HEAD_DIM), 0.02),
        'w_branch': nrm(41, (DEPTH, N_BRANCH, BRANCH_W, D_MODEL), BRANCH_W ** -0.5),
        'w_out': nrm(42, (DEPTH, D_MODEL, D_MODEL), D_MODEL ** -0.5),
        'ffn_w_up': nrm(43, (DEPTH, D_MODEL, 2 * D_FF), D_MODEL ** -0.5),
        'ffn_w_down': nrm(44, (DEPTH, D_FF, D_MODEL), D_FF ** -0.5),
    }


def reference(x_prompt, x_sample, c_prompt, c_sample, cache_moba_k, cache_moba_v, cache_sb_k, cache_sb_v,
              state_rwkv, state_rwkv_shift, state_s5_re, state_s5_im, page_table,
              norm_mix_g, norm_ffn_g, ada_w, ada_b, w_in, rwkv_mu, rwkv_w0, rwkv_w2, rwkv_a0, rwkv_a2,
              rwkv_g2, rwkv_k_k, rwkv_k_a, rwkv_r_k, rwkv_ln_g, rwkv_ln_b, s5_a_re, s5_a_im, s5_b_re,
              s5_b_im, s5_c_re, s5_c_im, s5_d, s5_log_dt, s5_w_glu, s5_b_glu, moba_q_norm, moba_k_norm,
              w_branch, w_out, ffn_w_up, ffn_w_down):
    f32 = jnp.float32
    bp, lp = x_prompt.shape[0], x_prompt.shape[1]
    bs, ls = x_sample.shape[0], x_sample.shape[1]
    past_len = page_table.shape[1] * cache_moba_k.shape[2]
    pos_p = jnp.arange(lp, dtype=jnp.int32)
    pos_s = past_len + jnp.arange(ls, dtype=jnp.int32)
    yp, ys = x_prompt, x_sample
    st_p_all, st_s_all = [], []
    for l in range(DEPTH):
        p = {
            'norm_mix_g': norm_mix_g[l], 'norm_ffn_g': norm_ffn_g[l], 'ada_w': ada_w[l], 'ada_b': ada_b[l],
            'w_in': w_in[l], 'rwkv_mu': rwkv_mu[l], 'rwkv_w0': rwkv_w0[l], 'rwkv_w2': rwkv_w2[l],
            'rwkv_a0': rwkv_a0[l], 'rwkv_a2': rwkv_a2[l], 'rwkv_g2': rwkv_g2[l], 'rwkv_k_k': rwkv_k_k[l],
            'rwkv_k_a': rwkv_k_a[l], 'rwkv_r_k': rwkv_r_k[l], 'rwkv_ln_g': rwkv_ln_g[l], 'rwkv_ln_b': rwkv_ln_b[l],
            's5_a_re': s5_a_re[l], 's5_a_im': s5_a_im[l], 's5_b_re': s5_b_re[l], 's5_b_im': s5_b_im[l],
            's5_c_re': s5_c_re[l], 's5_c_im': s5_c_im[l], 's5_d': s5_d[l], 's5_log_dt': s5_log_dt[l],
            's5_w_glu': s5_w_glu[l], 's5_b_glu': s5_b_glu[l], 'moba_q_norm': moba_q_norm[l],
            'moba_k_norm': moba_k_norm[l], 'w_branch': w_branch[l], 'w_out': w_out[l],
            'ffn_w_up': ffn_w_up[l], 'ffn_w_down': ffn_w_down[l],
        }
        yp, st_p = _layer(yp, c_prompt, pos_p,
                          jnp.zeros((bp, RWKV_IN), f32),
                          jnp.zeros((bp, RWKV_HEADS, HEAD_DIM, HEAD_DIM), f32),
                          jnp.zeros((bp, S5_GROUPS, S5_STATE), f32),
                          jnp.zeros((bp, S5_GROUPS, S5_STATE), f32),
                          None, None, None, None, p)
        ys, st_s = _layer(ys, c_sample, pos_s, state_rwkv_shift[l], state_rwkv[l],
                          state_s5_re[l], state_s5_im[l],
                          _gather_pages(cache_moba_k[l], page_table), _gather_pages(cache_moba_v[l], page_table),
                          _gather_pages(cache_sb_k[l], page_table), _gather_pages(cache_sb_v[l], page_table), p)
        st_p_all.append(st_p)
        st_s_all.append(st_s)
    stk = lambda lst, i: jnp.stack([s[i] for s in lst], axis=0)
    return (yp, ys,
            stk(st_p_all, 0), stk(st_p_all, 1), stk(st_s_all, 0), stk(st_s_all, 1),
            stk(st_p_all, 2), stk(st_p_all, 3), stk(st_s_all, 2), stk(st_s_all, 3),
            stk(st_p_all, 4), stk(st_s_all, 4),
            stk(st_p_all, 5), stk(st_s_all, 5),
            stk(st_p_all, 6), stk(st_p_all, 7), stk(st_s_all, 6), stk(st_s_all, 7))
```

```python
import functools
import math

import jax
import jax.numpy as jnp
import numpy as np
from jax import lax
from jax.experimental import pallas as pl
from jax.experimental.pallas import tpu as pltpu

F32 = jnp.float32
BF16 = jnp.bfloat16

HEAD_DIM = 64
N_BRANCH = 4
RWKV_DECAY_RANK = 64
RWKV_ICL_RANK = 64
RWKV_GATE_RANK = 128
RWKV_GN_EPS = 64e-5
S5_GROUP = 16
S5_STATE = 64
MOBA_BLOCK = 256
MOBA_TOPK = 3
ROT_DIM = HEAD_DIM // 4
ROPE_THETA = 500000.0
RMS_EPS = 1e-6

LANES = 128
SUBLANES = 8
VMEM_LIMIT_BYTES = 56 * 1024 * 1024
RWKV_CHUNK = 64
NEG_INF = float("-inf")


def _cp(sem):
    return pltpu.CompilerParams(dimension_semantics=sem, vmem_limit_bytes=VMEM_LIMIT_BYTES)


def _sigmoid(x):
    return 1.0 / (1.0 + jnp.exp(-x))


def _softplus(x):
    return jnp.maximum(x, 0.0) + jnp.log(1.0 + jnp.exp(-jnp.abs(x)))


def _gelu_tanh(x):
    return 0.5 * x * (1.0 + jnp.tanh(math.sqrt(2.0 / math.pi) * (x + 0.044715 * (x * x * x))))


def _mm(a, b):
    return jnp.dot(a.astype(BF16), b.astype(BF16), preferred_element_type=F32)


def _mm_nt(a, b):
    return lax.dot_general(a.astype(BF16), b.astype(BF16), (((1,), (1,)), ((), ())),
                           preferred_element_type=F32)


def _split(x, parts):
    out = []
    for _ in range(parts - 1):
        hi = x.astype(BF16)
        out.append(hi)
        x = x - hi.astype(F32)
    out.append(x.astype(BF16))
    return out


def _mm_exact_lhs(a_bf16, x, parts):
    acc = None
    for p in _split(x, parts):
        t = jnp.dot(a_bf16, p, preferred_element_type=F32)
        acc = t if acc is None else acc + t
    return acc


def _mm_exact_rhs(x, b_bf16, parts):
    acc = None
    for p in _split(x, parts):
        t = jnp.dot(p, b_bf16, preferred_element_type=F32)
        acc = t if acc is None else acc + t
    return acc


def _mm3(a, b):
    ah, al = _split(a, 2)
    bh, bl = _split(b, 2)
    return (jnp.dot(ah, bh, preferred_element_type=F32) + jnp.dot(ah, bl, preferred_element_type=F32)
            + jnp.dot(al, bh, preferred_element_type=F32))


def _norm_mod(x, g, sc, sh):
    ms = jnp.mean(x * x, axis=-1, keepdims=True)
    y = x * lax.rsqrt(ms + RMS_EPS)
    return (y * g) * (1.0 + sc) + sh


def _mod_spec(R, tm, D, ngrid):
    if ngrid == 3:
        if R == 1:
            return pl.BlockSpec((1, 1, D), lambda b, i, j: (b, 0, 0))
        return pl.BlockSpec((1, tm, D), lambda b, i, j: (b, i, 0))
    if R == 1:
        return pl.BlockSpec((1, 1, D), lambda b, i: (b, 0, 0))
    return pl.BlockSpec((1, tm, D), lambda b, i: (b, i, 0))


def _row_tile(L, pref):
    t = min(L, pref)
    assert L % t == 0, (L, t)
    return t


def _ada_kernel(c_ref, w_ref, b_ref, o_ref):
    c = c_ref[...]
    o_ref[...] = _mm(c * _sigmoid(c), w_ref[...]) + b_ref[...]


def _ada(c, w_bf16, b):
    rows, D = c.shape
    E = w_bf16.shape[1]
    tn = 1536
    assert E % tn == 0
    return pl.pallas_call(
        _ada_kernel, grid=(E // tn,),
        in_specs=[pl.BlockSpec((rows, D), lambda j: (0, 0)),
                  pl.BlockSpec((D, tn), lambda j: (0, j)),
                  pl.BlockSpec((1, tn), lambda j: (0, j))],
        out_specs=pl.BlockSpec((rows, tn), lambda j: (0, j)),
        out_shape=jax.ShapeDtypeStruct((rows, E), F32),
        compiler_params=_cp(("parallel",)), name="ada_mod")(c, w_bf16, b.reshape(1, E))


def _inproj_kernel(x_ref, g_ref, sc_ref, sh_ref, w_ref, o_ref, h_scr):
    @pl.when(pl.program_id(2) == 0)
    def _():
        h_scr[...] = _norm_mod(x_ref[0], g_ref[...], sc_ref[0], sh_ref[0]).astype(BF16)

    o_ref[0] = jnp.dot(h_scr[...], w_ref[...], preferred_element_type=F32)


def _inproj(x, g, sc, sh, w_bf16, name):
    Bn, L, D = x.shape
    E = w_bf16.shape[1]
    tm = _row_tile(L, 1024)
    tn = 256
    assert E % tn == 0
    R = sc.shape[1]
    ms = _mod_spec(R, tm, D, 3)
    return pl.pallas_call(
        _inproj_kernel, grid=(Bn, L // tm, E // tn),
        in_specs=[pl.BlockSpec((1, tm, D), lambda b, i, j: (b, i, 0)),
                  pl.BlockSpec((1, D), lambda b, i, j: (0, 0)),
                  ms, ms,
                  pl.BlockSpec((D, tn), lambda b, i, j: (0, j))],
        out_specs=pl.BlockSpec((1, tm, tn), lambda b, i, j: (b, i, j)),
        out_shape=jax.ShapeDtypeStruct((Bn, L, E), F32),
        scratch_shapes=[pltpu.VMEM((tm, D), BF16)],
        compiler_params=_cp(("parallel", "parallel", "arbitrary")), name=name)(x, g, sc, sh, w_bf16)


def _merge_kernel(oa_ref, ob_ref, oc_ref, od_ref, gate_ref, wb_ref, wo_ref, x_ref, gm_ref, o_ref, *, D):
    mixed = None
    for n, o in enumerate((oa_ref, ob_ref, oc_ref, od_ref)):
        br = _mm(o[0], wb_ref[n])
        t = _sigmoid(gate_ref[0, :, n * D:(n + 1) * D]) * br
        mixed = t if mixed is None else mixed + t
    o_ref[0] = x_ref[0] + gm_ref[0] * _mm(mixed, wo_ref[...])


def _merge(oa, ob, oc, od, gates, wb_bf16, wo_bf16, x, gm):
    Bn, L, D = x.shape
    W = oa.shape[2]
    tm = _row_tile(L, 256)
    R = gm.shape[1]
    osp = pl.BlockSpec((1, tm, W), lambda b, i: (b, i, 0))
    return pl.pallas_call(
        functools.partial(_merge_kernel, D=D), grid=(Bn, L // tm),
        in_specs=[osp, osp, osp, osp,
                  pl.BlockSpec((1, tm, N_BRANCH * D), lambda b, i: (b, i, 0)),
                  pl.BlockSpec((N_BRANCH, W, D), lambda b, i: (0, 0, 0)),
                  pl.BlockSpec((D, D), lambda b, i: (0, 0)),
                  pl.BlockSpec((1, tm, D), lambda b, i: (b, i, 0)),
                  _mod_spec(R, tm, D, 2)],
        out_specs=pl.BlockSpec((1, tm, D), lambda b, i: (b, i, 0)),
        out_shape=jax.ShapeDtypeStruct((Bn, L, D), F32),
        compiler_params=_cp(("parallel", "parallel")), name="branch_merge")(oa, ob, oc, od, gates, wb_bf16, wo_bf16, x, gm)


def _ffn_kernel(x_ref, g_ref, sc_ref, sh_ref, gf_ref, wg_ref, wv_ref, wd_ref, o_ref, h_scr, acc_scr):
    f = pl.program_id(2)

    @pl.when(f == 0)
    def _():
        h_scr[...] = _norm_mod(x_ref[0], g_ref[...], sc_ref[0], sh_ref[0]).astype(BF16)
        acc_scr[...] = jnp.zeros_like(acc_scr)

    h = h_scr[...]
    ug = jnp.dot(h, wg_ref[...], preferred_element_type=F32)
    uv = jnp.dot(h, wv_ref[...], preferred_element_type=F32)
    act = (ug * _sigmoid(ug)) * uv
    acc_scr[...] += _mm(act, wd_ref[...])

    @pl.when(f == pl.num_programs(2) - 1)
    def _():
        o_ref[0] = x_ref[0] + gf_ref[0] * acc_scr[...]


def _ffn(x, g, sc, sh, gf, wup_bf16, wdown_bf16):
    Bn, L, D = x.shape
    Fdim = wdown_bf16.shape[0]
    tm = _row_tile(L, 1024)
    tf = 256
    assert Fdim % tf == 0
    nf = Fdim // tf
    R = sc.shape[1]
    ms = _mod_spec(R, tm, D, 3)
    return pl.pallas_call(
        _ffn_kernel, grid=(Bn, L // tm, nf),
        in_specs=[pl.BlockSpec((1, tm, D), lambda b, i, f: (b, i, 0)),
                  pl.BlockSpec((1, D), lambda b, i, f: (0, 0)),
                  ms, ms, ms,
                  pl.BlockSpec((D, tf), lambda b, i, f: (0, f)),
                  pl.BlockSpec((D, tf), lambda b, i, f: (0, f + nf)),
                  pl.BlockSpec((tf, D), lambda b, i, f: (f, 0))],
        out_specs=pl.BlockSpec((1, tm, D), lambda b, i, f: (b, i, 0)),
        out_shape=jax.ShapeDtypeStruct((Bn, L, D), F32),
        scratch_shapes=[pltpu.VMEM((tm, D), BF16), pltpu.VMEM((tm, D), F32)],
        compiler_params=_cp(("parallel", "parallel", "arbitrary")), name="ffn")(
            x, g, sc, sh, gf, wup_bf16, wup_bf16, wdown_bf16)


def _segsum(x, ones_bd):
    return _mm_exact_rhs(x, ones_bd, 2)


def _rwkv_prep_kernel(pa_ref, prev_ref, mu_ref, w0_ref, a0_ref, kk_ref, ka_ref, w2_ref, a2_ref, g2_ref, ones_ref,
                      r_o, lw_o, kk_o, b_o, v_o, k_o, g_o, *, W):
    pa = pa_ref[0]
    xs = pa + (prev_ref[0] - pa) * mu_ref[...]
    r = xs[:, 0:W]
    k = xs[:, W:2 * W]
    v = xs[:, 2 * W:3 * W]
    wa = xs[:, 3 * W:3 * W + RWKV_DECAY_RANK + RWKV_ICL_RANK]
    gl = xs[:, 3 * W + RWKV_DECAY_RANK + RWKV_ICL_RANK:]
    w_log = -_softplus(-(w0_ref[...] + _mm(jnp.tanh(wa), w2_ref[...]))) - 0.5
    a = _sigmoid(a0_ref[...] + _mm(wa, a2_ref[...]))
    g = _mm(_sigmoid(gl), g2_ref[...])
    kkraw = k * kk_ref[...]
    n2 = _segsum(kkraw * kkraw, ones_ref[...])
    kk = kkraw / jnp.maximum(jnp.sqrt(n2), 1e-12)
    r_o[0] = r
    lw_o[0] = -jnp.exp(w_log)
    kk_o[0] = kk
    b_o[0] = kk * a
    v_o[0] = v
    k_o[0] = k * (1.0 + (a - 1.0) * ka_ref[...])
    g_o[0] = g


def _rwkv_prep(pa, prev, p):
    Bn, L, RIN = pa.shape
    W = p["W"]
    tm = _row_tile(L, 512)
    row = pl.BlockSpec((1, tm, RIN), lambda b, i: (b, i, 0))
    vec = lambda n: pl.BlockSpec((1, n), lambda b, i: (0, 0))
    mat = lambda s: pl.BlockSpec(s, lambda b, i: (0, 0))
    ospec = pl.BlockSpec((1, tm, W), lambda b, i: (b, i, 0))
    oshape = jax.ShapeDtypeStruct((Bn, L, W), F32)
    return pl.pallas_call(
        functools.partial(_rwkv_prep_kernel, W=W), grid=(Bn, L // tm),
        in_specs=[row, row, vec(RIN), vec(W), vec(W), vec(W), vec(W),
                  mat(p["w2pad"].shape), mat(p["a2pad"].shape), mat(p["g2"].shape), mat((W, W))],
        out_specs=[ospec] * 7, out_shape=[oshape] * 7,
        compiler_params=_cp(("parallel", "parallel")), name="rwkv_prep")(
            pa, prev, p["mu"], p["w0"], p["a0"], p["k_k"], p["k_a"], p["w2pad"], p["a2pad"], p["g2"], p["ones_bd"])


def _rwkv_chunk_kernel(r_ref, lw_ref, kk_ref, b_ref, v_ref, k_ref, h0_ref, y_ref, h_ref, *, T, H):
    @pl.when(pl.program_id(1) == 0)
    def _():
        h_ref[...] = h0_ref[...]

    row = lax.broadcasted_iota(jnp.int32, (T, T), 0)
    col = lax.broadcasted_iota(jnp.int32, (T, T), 1)
    strict = col < row
    incl = col <= row
    blk16 = (row // 16) == (col // 16)
    blk32 = (row // 32) == (col // 32)
    eye = jnp.where(row == col, 1.0, 0.0).astype(F32)

    lw = lw_ref[0]
    G = _mm_exact_lhs(incl.astype(BF16), lw, 3)
    GT = G[T - 1:T, :]
    eG = jnp.exp(G)
    einv = jnp.exp(-G)
    eTs = jnp.exp(GT - G)
    eGT = jnp.exp(GT)
    rt = r_ref[0] * eG
    kkt = kk_ref[0] * jnp.exp(G - lw)
    bh = b_ref[0] * einv
    kh = k_ref[0] * einv
    bT = b_ref[0] * eTs
    kT = k_ref[0] * eTs
    v = v_ref[0]

    ys = []
    for h in range(H):
        sl = slice(HEAD_DIM * h, HEAD_DIM * (h + 1))
        kkt_h, rt_h, bh_h, kh_h, v_h = kkt[:, sl], rt[:, sl], bh[:, sl], kh[:, sl], v[:, sl]
        a_ab = jnp.where(strict, _mm_nt(kkt_h, bh_h), 0.0)
        a_ak = jnp.where(strict, _mm_nt(kkt_h, kh_h), 0.0)
        a_rb = jnp.where(incl, _mm_nt(rt_h, bh_h), 0.0)
        a_rk = jnp.where(incl, _mm_nt(rt_h, kh_h), 0.0)
        n1 = jnp.where(blk16, -a_ab, 0.0)
        inv = eye + n1
        n2 = _mm(n1, n1)
        inv = inv + _mm(inv, n2)
        n4 = _mm(n2, n2)
        inv = inv + _mm(inv, n4)
        n8 = _mm(n4, n4)
        inv = inv + _mm(inv, n8)
        off32 = jnp.where(jnp.logical_and(blk32, jnp.logical_not(blk16)), a_ab, 0.0)
        inv = inv - _mm(_mm(inv, off32), inv)
        off64 = jnp.where(blk32, 0.0, a_ab)
        inv = inv - _mm(_mm(inv, off64), inv)

        hh = h_ref[0, h]
        u = -_mm(inv, _mm(kkt_h, hh) + _mm(a_ak, v_h))
        ys.append(_mm(rt_h, hh) + _mm(a_rb, u) + _mm(a_rk, v_h))
        decay_col = jnp.broadcast_to(eGT[:, sl], (HEAD_DIM, HEAD_DIM)).T
        h_ref[0, h] = decay_col * hh + _mm(bT[:, sl].T, u) + _mm(kT[:, sl].T, v_h)
    y_ref[0] = jnp.concatenate(ys, axis=1)


def _rwkv_chunk(r, lw, kk, b, v, k, h0):
    Bn, L, W = r.shape
    H = W // HEAD_DIM
    T = RWKV_CHUNK
    assert L % T == 0
    row = pl.BlockSpec((1, T, W), lambda bi, c: (bi, c, 0))
    st = pl.BlockSpec((1, H, HEAD_DIM, HEAD_DIM), lambda bi, c: (bi, 0, 0, 0))
    return pl.pallas_call(
        functools.partial(_rwkv_chunk_kernel, T=T, H=H), grid=(Bn, L // T),
        in_specs=[row] * 6 + [st],
        out_specs=[row, st],
        out_shape=[jax.ShapeDtypeStruct((Bn, L, W), F32),
                   jax.ShapeDtypeStruct((Bn, H, HEAD_DIM, HEAD_DIM), F32)],
        compiler_params=_cp(("parallel", "arbitrary")), name="rwkv_chunk")(r, lw, kk, b, v, k, h0)


def _rwkv_step_kernel(s_ref, r_ref, lw_ref, kk_ref, b_ref, v_ref, k_ref, y_ref, so_ref):
    S = s_ref[0]
    r, kk, b, v, k = r_ref[0], kk_ref[0], b_ref[0], v_ref[0], k_ref[0]
    w = jnp.exp(lw_ref[0])
    i0 = lax.broadcasted_iota(jnp.int32, (HEAD_DIM, HEAD_DIM), 0)
    i1 = lax.broadcasted_iota(jnp.int32, (HEAD_DIM, HEAD_DIM), 1)
    eye = jnp.where(i0 == i1, 1.0, 0.0).astype(F32)
    sa = jnp.sum(S * kk, axis=-1, keepdims=True)
    vcol = jnp.sum(eye * v, axis=-1, keepdims=True)
    Sn = S * w - sa * b + vcol * k
    ycol = jnp.sum(Sn * r, axis=-1, keepdims=True)
    y_ref[0] = jnp.sum(eye * ycol, axis=-2, keepdims=True)
    so_ref[0] = Sn


def _rwkv_step(S0, r, lw, kk, b, v, k):
    Bn, H = S0.shape[0], S0.shape[1]
    hv = lambda t: t.reshape(Bn, H, 1, HEAD_DIM)
    vs = pl.BlockSpec((1, H, 1, HEAD_DIM), lambda bi: (bi, 0, 0, 0))
    ss = pl.BlockSpec((1, H, HEAD_DIM, HEAD_DIM), lambda bi: (bi, 0, 0, 0))
    y, Sn = pl.pallas_call(
        _rwkv_step_kernel, grid=(Bn,),
        in_specs=[ss] + [vs] * 6, out_specs=[vs, ss],
        out_shape=[jax.ShapeDtypeStruct((Bn, H, 1, HEAD_DIM), F32),
                   jax.ShapeDtypeStruct((Bn, H, HEAD_DIM, HEAD_DIM), F32)],
        compiler_params=_cp(("parallel",)), name="rwkv_step")(S0, hv(r), hv(lw), hv(kk), hv(b), hv(v), hv(k))
    return y.reshape(Bn, H * HEAD_DIM), Sn


def _rwkv_post_kernel(y_ref, r_ref, k_ref, v_ref, g_ref, lng_ref, lnb_ref, rk_ref, ones_ref, o_ref):
    ones = ones_ref[...]
    y = y_ref[0]
    inv_n = 1.0 / HEAD_DIM
    mu = _segsum(y, ones) * inv_n
    d = y - mu
    var = _segsum(d * d, ones) * inv_n
    yn = d * lax.rsqrt(var + RWKV_GN_EPS) * lng_ref[...] + lnb_ref[...]
    bonus = _segsum(r_ref[0] * k_ref[0] * rk_ref[...], ones) * v_ref[0]
    o_ref[0] = (yn + bonus) * g_ref[0]


def _rwkv_post(y, r, k, v, g, p):
    Bn, L, W = y.shape
    tm = _row_tile(L, 512)
    row = pl.BlockSpec((1, tm, W), lambda b, i: (b, i, 0))
    vec = pl.BlockSpec((1, W), lambda b, i: (0, 0))
    return pl.pallas_call(
        _rwkv_post_kernel, grid=(Bn, L // tm),
        in_specs=[row] * 5 + [vec] * 3 + [pl.BlockSpec((W, W), lambda b, i: (0, 0))],
        out_specs=row, out_shape=jax.ShapeDtypeStruct((Bn, L, W), F32),
        compiler_params=_cp(("parallel", "parallel")), name="rwkv_post")(
            y, r, k, v, g, p["ln_g"], p["ln_b"], p["r_k"], p["ones_bd"])


def _s5_in(u, wb_re_ref, wb_im_ref, nblk):
    ub = u.astype(BF16)
    res_re, res_im = [], []
    for a in range(nblk):
        ua = ub[:, LANES * a:LANES * (a + 1)]
        res_re.append(jnp.dot(ua, wb_re_ref[a], preferred_element_type=F32))
        res_im.append(jnp.dot(ua, wb_im_ref[a], preferred_element_type=F32))
    return jnp.concatenate(res_re, axis=1), jnp.concatenate(res_im, axis=1)


def _s5_out(h_re, h_im, u, wc_re_ref, wc_im_ref, d_ref, wglu_ref, bglu_ref, nblk):
    hr = h_re.astype(BF16)
    hi = h_im.astype(BF16)
    spb = hr.shape[1] // nblk
    ys = []
    for a in range(nblk):
        sl = slice(spb * a, spb * (a + 1))
        ys.append(jnp.dot(hr[:, sl], wc_re_ref[a], preferred_element_type=F32)
                  - jnp.dot(hi[:, sl], wc_im_ref[a], preferred_element_type=F32))
    y = jnp.concatenate(ys, axis=1) + d_ref[...] * u
    yg = _gelu_tanh(y)
    return yg * _sigmoid(_mm(yg, wglu_ref[...]) + bglu_ref[...])


def _s5_scan_kernel(u_ref, h0re_ref, h0im_ref, wbre_ref, wbim_ref, wcre_ref, wcim_ref, d_ref, wglu_ref, bglu_ref,
                    p2re_ref, p2im_ref, p8re_ref, p8im_ref, o_ref, hre_o, him_o, *, T, nblk):
    @pl.when(pl.program_id(1) == 0)
    def _():
        hre_o[0] = h0re_ref[0]
        him_o[0] = h0im_ref[0]

    u = u_ref[0]
    bre, bim = _s5_in(u, wbre_ref, wbim_ref, nblk)
    NS = bre.shape[1]
    ng = T // SUBLANES
    xr = bre.reshape(ng, SUBLANES, NS)
    xi = bim.reshape(ng, SUBLANES, NS)
    sub = lax.broadcasted_iota(jnp.int32, (1, SUBLANES, 1), 1)
    for s, d in enumerate((1, 2, 4)):
        ar = p2re_ref[s:s + 1, :].reshape(1, 1, NS)
        ai = p2im_ref[s:s + 1, :].reshape(1, 1, NS)
        keep = sub >= d
        sr = jnp.where(keep, pltpu.roll(xr, d, 1), 0.0)
        si = jnp.where(keep, pltpu.roll(xi, d, 1), 0.0)
        xr, xi = xr + ar * sr - ai * si, xi + ar * si + ai * sr
    p8r = p8re_ref[...]
    p8i = p8im_ref[...]
    cr = hre_o[0]
    ci = him_o[0]
    outs_r, outs_i = [], []
    for gidx in range(ng):
        hr = xr[gidx] + p8r * cr - p8i * ci
        hi = xi[gidx] + p8r * ci + p8i * cr
        outs_r.append(hr)
        outs_i.append(hi)
        cr = hr[SUBLANES - 1:SUBLANES, :]
        ci = hi[SUBLANES - 1:SUBLANES, :]
    hre_o[0] = cr
    him_o[0] = ci
    h_re = jnp.concatenate(outs_r, axis=0)
    h_im = jnp.concatenate(outs_i, axis=0)
    o_ref[0] = _s5_out(h_re, h_im, u, wcre_ref, wcim_ref, d_ref, wglu_ref, bglu_ref, nblk)


def _s5_prompt(u, p):
    Bn, L, W = u.shape
    NS = p["ab_re"].shape[1]
    T = _row_tile(L, 128)
    nblk = W // LANES
    row = pl.BlockSpec((1, T, W), lambda b, c: (b, c, 0))
    st = pl.BlockSpec((1, 1, NS), lambda b, c: (b, 0, 0))
    cst = lambda a: pl.BlockSpec(a.shape, lambda b, c: (0,) * a.ndim)
    h0 = jnp.zeros((Bn, 1, NS), F32)
    consts = [p[k] for k in ("wb_re", "wb_im", "wc_re", "wc_im", "d", "w_glu", "b_glu",
                             "pow2_re", "pow2_im", "pow8_re", "pow8_im")]
    return pl.pallas_call(
        functools.partial(_s5_scan_kernel, T=T, nblk=nblk), grid=(Bn, L // T),
        in_specs=[row, st, st] + [cst(a) for a in consts],
        out_specs=[row, st, st],
        out_shape=[jax.ShapeDtypeStruct((Bn, L, W), F32),
                   jax.ShapeDtypeStruct((Bn, 1, NS), F32), jax.ShapeDtypeStruct((Bn, 1, NS), F32)],
        compiler_params=_cp(("parallel", "arbitrary")), name="s5_scan")(u, h0, h0, *consts)


def _s5_step_kernel(u_ref, h0re_ref, h0im_ref, abre_ref, abim_ref, wbre_ref, wbim_ref, wcre_ref, wcim_ref,
                    d_ref, wglu_ref, bglu_ref, o_ref, hre_o, him_o, *, nblk):
    u = u_ref[...]
    bre, bim = _s5_in(u, wbre_ref, wbim_ref, nblk)
    ar, ai = abre_ref[...], abim_ref[...]
    r0, i0 = h0re_ref[...], h0im_ref[...]
    h_re = bre + ar * r0 - ai * i0
    h_im = bim + ar * i0 + ai * r0
    hre_o[...] = h_re
    him_o[...] = h_im
    o_ref[...] = _s5_out(h_re, h_im, u, wcre_ref, wcim_ref, d_ref, wglu_ref, bglu_ref, nblk)


def _s5_step(u, h0_re, h0_im, p):
    rows, W = u.shape
    NS = p["ab_re"].shape[1]
    nblk = W // LANES
    consts = [p[k] for k in ("ab_re", "ab_im", "wb_re", "wb_im", "wc_re", "wc_im", "d", "w_glu", "b_glu")]
    args = [u, h0_re, h0_im] + consts
    full = lambda a: pl.BlockSpec(a.shape, lambda i: (0,) * a.ndim)
    return pl.pallas_call(
        functools.partial(_s5_step_kernel, nblk=nblk), grid=(1,),
        in_specs=[full(a) for a in args],
        out_specs=[full(u), full(h0_re), full(h0_im)],
        out_shape=[jax.ShapeDtypeStruct((rows, W), F32), jax.ShapeDtypeStruct((rows, NS), F32),
                   jax.ShapeDtypeStruct((rows, NS), F32)],
        compiler_params=_cp(("arbitrary",)), name="s5_step")(*args)


def _attn_prep_kernel(*refs, normrope, emit_kv, W):
    refs = list(refs)
    q_ref, k_ref, v_ref = refs[:3]
    if normrope:
        qn_ref, kn_ref, cos_ref, sin_ref, ones_ref = refs[3:8]
    outs = refs[8:] if normrope else refs[3:]
    q_o = outs.pop(0)
    if normrope:
        k_o = outs.pop(0)
    if emit_kv:
        kb_o, vt_o = outs[0], outs[1]
        if normrope:
            km_o = outs[2]
    q, k = q_ref[0], k_ref[0]
    if normrope:
        ones = ones_ref[...]
        cosf, sinf = cos_ref[...], sin_ref[...]
        dmod = lax.broadcasted_iota(jnp.int32, (1, W), 1) % HEAD_DIM
        half = ROT_DIM // 2

        def nr(x, g):
            ms = _segsum(x * x, ones) * (1.0 / HEAD_DIM)
            xn = x * lax.rsqrt(ms + RMS_EPS) * g
            up = pltpu.roll(xn, W - half, 1)
            dn = pltpu.roll(xn, half, 1)
            rot = jnp.where(dmod < half, -up, jnp.where(dmod < ROT_DIM, dn, 0.0))
            return xn * cosf + rot * sinf

        q = nr(q, qn_ref[...])
        k = nr(k, kn_ref[...])
        k_o[0] = k
    q_o[0] = q
    if emit_kv:
        kb_o[0] = k.astype(BF16)
        vt_o[0] = v_ref[0].T.astype(BF16)
        if normrope:
            km_o[0, 0] = jnp.mean(k, axis=0, keepdims=True)


def _attn_prep(pq, normrope, emit_kv, p=None, cosf=None, sinf=None):
    Bn, L, W3 = pq.shape
    W = W3 // 3
    tm = _row_tile(L, MOBA_BLOCK)
    nbk = L // tm
    col = lambda c: pl.BlockSpec((1, tm, W), lambda b, i, c=c: (b, i, c))
    row_o = pl.BlockSpec((1, tm, W), lambda b, i: (b, i, 0))
    vt_o = pl.BlockSpec((1, W, tm), lambda b, i: (b, 0, i))
    in_specs = [col(0), col(1), col(2)]
    args = [pq, pq, pq]
    out_specs = [row_o]
    out_shape = [jax.ShapeDtypeStruct((Bn, L, W), F32)]
    if normrope:
        vec = pl.BlockSpec((1, W), lambda b, i: (0, 0))
        tab = pl.BlockSpec((tm, W), lambda b, i: (i, 0))
        in_specs += [vec, vec, tab, tab, pl.BlockSpec((W, W), lambda b, i: (0, 0))]
        args += [p["q_norm"], p["k_norm"], cosf, sinf, p["ones_bd"]]
        out_specs.append(row_o)
        out_shape.append(jax.ShapeDtypeStruct((Bn, L, W), F32))
    if emit_kv:
        out_specs += [row_o, vt_o]
        out_shape += [jax.ShapeDtypeStruct((Bn, L, W), BF16), jax.ShapeDtypeStruct((Bn, W, L), BF16)]
        if normrope:
            out_specs.append(pl.BlockSpec((1, 1, 1, W), lambda b, i: (b, i, 0, 0)))
            out_shape.append(jax.ShapeDtypeStruct((Bn, nbk, 1, W), F32))
    return pl.pallas_call(
        functools.partial(_attn_prep_kernel, normrope=normrope, emit_kv=emit_kv, W=W), grid=(Bn, nbk),
        in_specs=in_specs, out_specs=out_specs, out_shape=out_shape,
        compiler_params=_cp(("parallel", "parallel")),
        name="moba_prep" if normrope else "sb_prep")(*args)


def _moba_kernel(q_ref, kb_ref, vt_ref, km_ref, o_ref, m_scr, l_scr, acc_scr, sel_scr, *, BQ, NB):
    i = pl.program_id(2)
    qT = q_ref[0].T
    hrow = lax.broadcasted_iota(jnp.int32, (LANES, 1), 0) // HEAD_DIM
    km = km_ref[0]
    nidx = lax.broadcasted_iota(jnp.int32, (NB, 1), 0)
    kpos = lax.broadcasted_iota(jnp.int32, (BQ, BQ), 0)
    qpos = lax.broadcasted_iota(jnp.int32, (BQ, BQ), 1)
    scale = HEAD_DIM ** -0.5
    qms = []
    for hh in range(2):
        qm = jnp.where(hrow == hh, qT, 0.0)
        gate = _mm3(km, qm)
        valid = nidx < i
        g1 = jnp.where(valid, gate, NEG_INF)
        m1 = jnp.max(g1, axis=0, keepdims=True)
        g2 = jnp.where(g1 >= m1, NEG_INF, g1)
        m2 = jnp.max(g2, axis=0, keepdims=True)
        g3 = jnp.where(g2 >= m2, NEG_INF, g2)
        m3 = jnp.max(g3, axis=0, keepdims=True)
        sel_scr[hh] = jnp.where(jnp.logical_and(valid, gate >= m3), 1.0, 0.0)
        qmb = (qm * scale).astype(BF16)
        qms.append(qmb)
        kj = kb_ref[0, pl.ds(i * BQ, BQ), :]
        s = jnp.dot(kj, qmb, preferred_element_type=F32)
        s = jnp.where(kpos <= qpos, s, NEG_INF)
        m = jnp.max(s, axis=0, keepdims=True)
        pexp = jnp.exp(s - m)
        m_scr[hh] = m
        l_scr[hh] = jnp.sum(pexp, axis=0, keepdims=True)
        acc_scr[hh] = jnp.dot(vt_ref[0, :, pl.ds(i * BQ, BQ)], pexp.astype(BF16), preferred_element_type=F32)

    def body(j, carry):
        kj = kb_ref[0, pl.ds(j * BQ, BQ), :]
        vj = vt_ref[0, :, pl.ds(j * BQ, BQ)]
        for hh in range(2):
            s = jnp.dot(kj, qms[hh], preferred_element_type=F32)
            selrow = sel_scr[hh, pl.ds(j, 1), :]
            s = jnp.where(selrow > 0.5, s, NEG_INF)
            m_old = m_scr[hh]
            m_new = jnp.maximum(m_old, jnp.max(s, axis=0, keepdims=True))
            alpha = jnp.exp(m_old - m_new)
            pexp = jnp.exp(s - m_new)
            m_scr[hh] = m_new
            l_scr[hh] = alpha * l_scr[hh] + jnp.sum(pexp, axis=0, keepdims=True)
            acc_scr[hh] = alpha * acc_scr[hh] + jnp.dot(vj, pexp.astype(BF16), preferred_element_type=F32)
        return carry

    lax.fori_loop(0, i, body, 0)
    o0 = acc_scr[0] / l_scr[0]
    o1 = acc_scr[1] / l_scr[1]
    o_ref[0] = jnp.where(hrow == 0, o0, o1).T


def _moba_prompt(q, kb, vt, kmean):
    Bn, L, W = q.shape
    BQ = MOBA_BLOCK
    assert L % BQ == 0
    NB = L // BQ
    NBP = -(-NB // SUBLANES) * SUBLANES
    km = kmean.reshape(Bn, NB, W)
    if NBP != NB:
        km = jnp.pad(km, ((0, 0), (0, NBP - NB), (0, 0)))
    npair = W // LANES
    return pl.pallas_call(
        functools.partial(_moba_kernel, BQ=BQ, NB=NBP), grid=(Bn, npair, NB),
        in_specs=[pl.BlockSpec((1, BQ, LANES), lambda b, p, i: (b, i, p)),
                  pl.BlockSpec((1, L, LANES), lambda b, p, i: (b, 0, p)),
                  pl.BlockSpec((1, LANES, L), lambda b, p, i: (b, p, 0)),
                  pl.BlockSpec((1, NBP, LANES), lambda b, p, i: (b, 0, p))],
        out_specs=pl.BlockSpec((1, BQ, LANES), lambda b, p, i: (b, i, p)),
        out_shape=jax.ShapeDtypeStruct((Bn, L, W), F32),
        scratch_shapes=[pltpu.VMEM((2, 1, BQ), F32), pltpu.VMEM((2, 1, BQ), F32),
                        pltpu.VMEM((2, LANES, BQ), F32), pltpu.VMEM((2, NBP, BQ), F32)],
        compiler_params=_cp(("parallel", "parallel", "arbitrary")), name="moba_attn")(q, kb, vt, km)


def _sb_terms(z):
    sp = jnp.log(1.0 + jnp.exp(-jnp.abs(z)))
    return jnp.minimum(z, 0.0) - sp, jnp.minimum(-z, 0.0) - sp


def _sb_kernel(q_ref, kb_ref, vt_ref, o_ref, carry_scr, acc_scr, *, BQ):
    i = pl.program_id(2)
    qT = q_ref[0].T
    hrow = lax.broadcasted_iota(jnp.int32, (LANES, 1), 0) // HEAD_DIM
    kpos = lax.broadcasted_iota(jnp.int32, (BQ, BQ), 0)
    qpos = lax.broadcasted_iota(jnp.int32, (BQ, BQ), 1)
    later = (qpos > kpos).astype(BF16)
    scale = HEAD_DIM ** -0.5
    qms = [(jnp.where(hrow == hh, qT, 0.0) * scale).astype(BF16) for hh in range(2)]

    def block(j, diag):
        kj = kb_ref[0, pl.ds(j * BQ, BQ), :]
        vj = vt_ref[0, :, pl.ds(j * BQ, BQ)]
        for hh in range(2):
            z = jnp.dot(kj, qms[hh], preferred_element_type=F32)
            ls, lneg = _sb_terms(z)
            if diag:
                ok = kpos < qpos
                lneg = jnp.where(ok, lneg, 0.0)
            carry = carry_scr[hh]
            between = _mm_exact_lhs(later, lneg, 2) + carry
            wgt = jnp.exp(ls + between)
            if diag:
                wgt = jnp.where(ok, wgt, 0.0)
            carry_scr[hh] = carry + jnp.sum(lneg, axis=0, keepdims=True)
            acc_scr[hh] = acc_scr[hh] + jnp.dot(vj, wgt.astype(BF16), preferred_element_type=F32)

    carry_scr[...] = jnp.zeros_like(carry_scr)
    acc_scr[...] = jnp.zeros_like(acc_scr)
    block(i, True)

    def body(t, c):
        block(i - 1 - t, False)
        return c

    lax.fori_loop(0, i, body, 0)
    o_ref[0] = jnp.where(hrow == 0, acc_scr[0], acc_scr[1]).T


def _sb_prompt(q, kb, vt):
    Bn, L, W = q.shape
    BQ = _row_tile(L, MOBA_BLOCK)
    npair = W // LANES
    return pl.pallas_call(
        functools.partial(_sb_kernel, BQ=BQ), grid=(Bn, npair, L // BQ),
        in_specs=[pl.BlockSpec((1, BQ, LANES), lambda b, p, i: (b, i, p)),
                  pl.BlockSpec((1, L, LANES), lambda b, p, i: (b, 0, p)),
                  pl.BlockSpec((1, LANES, L), lambda b, p, i: (b, p, 0))],
        out_specs=pl.BlockSpec((1, BQ, LANES), lambda b, p, i: (b, i, p)),
        out_shape=jax.ShapeDtypeStruct((Bn, L, W), F32),
        scratch_shapes=[pltpu.VMEM((2, 1, BQ), F32), pltpu.VMEM((2, LANES, BQ), F32)],
        compiler_params=_cp(("parallel", "parallel", "arbitrary")), name="sb_attn")(q, kb, vt)


PAGES_PER_STEP = 8


def _page_specs(n, page, W, nsteps, order_desc):
    specs = []
    for t in range(n):
        if order_desc:
            specs.append(pl.BlockSpec(
                (1, page, W), lambda b, s, pt, t=t: (pt[b, (nsteps - 1 - s) * n + t], 0, 0)))
        else:
            specs.append(pl.BlockSpec((1, page, W), lambda b, s, pt, t=t: (pt[b, s * n + t], 0, 0)))
    return specs


def _page_sum_kernel(pt_ref, *refs):
    n = len(refs) - 1
    o_ref = refs[n]
    o_ref[0] = jnp.concatenate([jnp.sum(refs[t][0], axis=0, keepdims=True) for t in range(n)], axis=0)


def _page_sums(pool, page_table):
    NP, page, W = pool.shape
    DB, n_pages = page_table.shape
    n = PAGES_PER_STEP
    assert n_pages % n == 0
    gs = pltpu.PrefetchScalarGridSpec(
        num_scalar_prefetch=1, grid=(DB, n_pages // n),
        in_specs=_page_specs(n, page, W, n_pages // n, False),
        out_specs=pl.BlockSpec((1, n, W), lambda b, s, pt: (b, s, 0)))
    return pl.pallas_call(
        _page_sum_kernel, grid_spec=gs, out_shape=jax.ShapeDtypeStruct((DB, n_pages, W), F32),
        compiler_params=_cp(("parallel", "arbitrary")), name="moba_page_sums")(page_table, *([pool] * n))


def _moba_gate_kernel(ks_ref, q_ref, seg_ref, idx_ref, *, NBK, W, inv_cnt):
    ks = ks_ref[0]
    ppb = ks.shape[1] // W
    tot = ks[:, 0:W]
    for t in range(1, ppb):
        tot = tot + ks[:, t * W:(t + 1) * W]
    kmean = tot * inv_cnt
    gate = _mm_exact_rhs(kmean * q_ref[0], seg_ref[...], 3)
    nidx = lax.broadcasted_iota(jnp.int32, gate.shape, 0)
    rows = []
    g = gate
    for _ in range(MOBA_TOPK):
        m = jnp.max(g, axis=0, keepdims=True)
        am = jnp.min(jnp.where(g >= m, nidx, NBK), axis=0, keepdims=True)
        rows.append(am)
        g = jnp.where(nidx == am, NEG_INF, g)
    rows += [jnp.zeros_like(rows[0])] * (SUBLANES - MOBA_TOPK)
    idx_ref[0] = jnp.concatenate(rows, axis=0)


def _moba_sample_select(ksums, q, seg, pages_per_block):
    DB, n_pages, W = ksums.shape
    NBK = n_pages // pages_per_block
    ks = ksums.reshape(DB, NBK, pages_per_block * W)
    inv_cnt = 1.0 / MOBA_BLOCK
    return pl.pallas_call(
        functools.partial(_moba_gate_kernel, NBK=NBK, W=W, inv_cnt=inv_cnt), grid=(DB,),
        in_specs=[pl.BlockSpec((1, NBK, pages_per_block * W), lambda b: (b, 0, 0)),
                  pl.BlockSpec((1, 1, W), lambda b: (b, 0, 0)),
                  pl.BlockSpec((W, LANES), lambda b: (0, 0))],
        out_specs=pl.BlockSpec((1, SUBLANES, LANES), lambda b: (b, 0, 0)),
        out_shape=jax.ShapeDtypeStruct((DB, SUBLANES, LANES), jnp.int32),
        compiler_params=_cp(("parallel",)), name="moba_sample_gate")(ks, q.reshape(DB, 1, W), seg)


def _moba_sample_attn_kernel(sel_ref, pt_ref, *refs, n_pg):
    q_ref, kn_ref, vn_ref, o_ref = refs[4 * n_pg:]
    lane = lax.broadcasted_iota(jnp.int32, (1, LANES), 1) // HEAD_DIM
    scale = HEAD_DIM ** -0.5
    out = jnp.zeros((1, LANES), F32)
    for hh in range(2):
        hm = lane == hh
        qh = jnp.where(hm, q_ref[0], 0.0) * scale
        kp = [refs[hh * 2 * n_pg + t][0] for t in range(n_pg)]
        vp = [refs[hh * 2 * n_pg + n_pg + t][0] for t in range(n_pg)]
        s_new = jnp.sum(qh * kn_ref[0], axis=-1, keepdims=True)
        ss = [jnp.sum(kpg * qh, axis=-1, keepdims=True) for kpg in kp]
        m = s_new
        for s in ss:
            m = jnp.maximum(m, jnp.max(s, axis=0, keepdims=True))
        p_new = jnp.exp(s_new - m)
        den = p_new
        acc = p_new * vn_ref[0]
        for s, vpg in zip(ss, vp):
            pe = jnp.exp(s - m)
            den = den + jnp.sum(pe, axis=0, keepdims=True)
            acc = acc + jnp.sum(pe * vpg, axis=0, keepdims=True)
        out = jnp.where(hm, acc / den, out)
    o_ref[0] = out


def _moba_sample_attn(sel, page_table, pool_k, pool_v, q, k_new, v_new, pages_per_block):
    DB, H, topk = sel.shape
    NP, page, W = pool_k.shape
    npair = W // LANES
    n_pg = topk * pages_per_block

    def pspec(hh, t):
        r, half = divmod(t, pages_per_block)
        return pl.BlockSpec(
            (1, page, LANES),
            lambda b, p, sel_r, pt, hh=hh, r=r, half=half:
            (pt[b, sel_r[b, 2 * p + hh, r] * pages_per_block + half], 0, p))

    in_specs, args = [], []
    for hh in range(2):
        for pool in (pool_k, pool_v):
            for t in range(n_pg):
                in_specs.append(pspec(hh, t))
                args.append(pool)
    vec = pl.BlockSpec((1, 1, LANES), lambda b, p, sel_r, pt: (b, 0, p))
    in_specs += [vec, vec, vec]
    r3 = lambda t: t.reshape(DB, 1, W)
    args += [r3(q), r3(k_new), r3(v_new)]
    gs = pltpu.PrefetchScalarGridSpec(num_scalar_prefetch=2, grid=(DB, npair), in_specs=in_specs, out_specs=vec)
    out = pl.pallas_call(
        functools.partial(_moba_sample_attn_kernel, n_pg=n_pg), grid_spec=gs,
        out_shape=jax.ShapeDtypeStruct((DB, 1, W), F32),
        compiler_params=_cp(("parallel", "arbitrary")), name="moba_sample_attn")(sel, page_table, *args)
    return out.reshape(DB, W)


def _sb_sample_kernel(pt_ref, *refs, n, page, W):
    k_refs, v_refs = refs[:n], refs[n:2 * n]
    qbd_ref, segT_ref, o_ref, carry_scr, acc_scr = refs[2 * n:]
    s = pl.program_id(1)

    @pl.when(s == 0)
    def _():
        carry_scr[...] = jnp.zeros_like(carry_scr)
        acc_scr[...] = jnp.zeros_like(acc_scr)

    r0 = lax.broadcasted_iota(jnp.int32, (page, page), 0)
    r1 = lax.broadcasted_iota(jnp.int32, (page, page), 1)
    later = (r1 > r0).astype(BF16)
    qbd = qbd_ref[0]
    segT = segT_ref[...]
    carry = carry_scr[...]
    acc = acc_scr[...]
    for t in reversed(range(n)):
        kpg = k_refs[t][0]
        z = jnp.dot(kpg.astype(BF16), qbd, preferred_element_type=F32)
        ls, lneg = _sb_terms(z)
        between = _mm_exact_lhs(later, lneg, 2) + carry
        wgt = jnp.exp(ls + between)
        carry = carry + jnp.sum(lneg, axis=0, keepdims=True)
        wexp = jnp.dot(wgt.astype(BF16), segT, preferred_element_type=F32)
        prod = wexp * v_refs[t][0]
        acc = acc + jnp.sum(prod.reshape(page // SUBLANES, SUBLANES, W), axis=0)
    carry_scr[...] = carry
    acc_scr[...] = acc

    @pl.when(s == pl.num_programs(1) - 1)
    def _():
        o_ref[0] = jnp.sum(acc, axis=0, keepdims=True)


def _sb_sample(pool_k, pool_v, page_table, q, seg):
    NP, page, W = pool_k.shape
    DB, n_pages = page_table.shape
    n = PAGES_PER_STEP
    assert n_pages % n == 0
    H = W // HEAD_DIM
    scale = HEAD_DIM ** -0.5
    qbd = ((q * scale)[:, :, None] * seg.astype(F32)[None, :, :]).astype(BF16)
    segT = seg.T
    in_specs = (_page_specs(n, page, W, n_pages // n, True) + _page_specs(n, page, W, n_pages // n, True)
                + [pl.BlockSpec((1, W, LANES), lambda b, s, pt: (b, 0, 0)),
                   pl.BlockSpec((LANES, W), lambda b, s, pt: (0, 0))])
    gs = pltpu.PrefetchScalarGridSpec(
        num_scalar_prefetch=1, grid=(DB, n_pages // n), in_specs=in_specs,
        out_specs=pl.BlockSpec((1, 1, W), lambda b, s, pt: (b, 0, 0)),
        scratch_shapes=[pltpu.VMEM((1, LANES), F32), pltpu.VMEM((SUBLANES, W), F32)])
    out = pl.pallas_call(
        functools.partial(_sb_sample_kernel, n=n, page=page, W=W), grid_spec=gs,
        out_shape=jax.ShapeDtypeStruct((DB, 1, W), F32),
        compiler_params=_cp(("parallel", "arbitrary")), name="sb_sample")(
            page_table, *([pool_k] * n), *([pool_v] * n), qbd, segT)
    return out.reshape(DB, W)


def _rope_tables(pos, W):
    half = ROT_DIM // 2
    inv = ROPE_THETA ** (-jnp.arange(half, dtype=F32) * 2.0 / ROT_DIM)
    ang = pos.astype(F32)[:, None] * inv[None, :]
    n = pos.shape[0]
    pad_c = jnp.ones((n, HEAD_DIM - ROT_DIM), F32)
    pad_s = jnp.zeros((n, HEAD_DIM - ROT_DIM), F32)
    cos_h = jnp.concatenate([jnp.cos(ang), jnp.cos(ang), pad_c], axis=1)
    sin_h = jnp.concatenate([jnp.sin(ang), jnp.sin(ang), pad_s], axis=1)
    reps = W // HEAD_DIM
    return jnp.tile(cos_h, (1, reps)), jnp.tile(sin_h, (1, reps))


def _block_diag(blocks):
    G, a, b = blocks.shape
    eye = jnp.eye(G, dtype=blocks.dtype)
    return (blocks[:, :, None, :] * eye[:, None, :, None]).reshape(G * a, G * b)


def _s5_params(l, a_re, a_im, b_re, b_im, c_re, c_im, d, log_dt, w_glu, b_glu, W):
    G, P = a_re.shape[1], a_re.shape[2]
    C = W // G
    are, aim = a_re[l], a_im[l]
    dt = jnp.exp(log_dt[l])[:, None]
    mag = jnp.exp(are * dt)
    ab_re, ab_im = mag * jnp.cos(aim * dt), mag * jnp.sin(aim * dt)
    den = are * are + aim * aim
    n_re = ab_re - 1.0
    cf_re = (n_re * are + ab_im * aim) / den
    cf_im = (ab_im * are - n_re * aim) / den
    bb_re = cf_re[..., None] * b_re[l] - cf_im[..., None] * b_im[l]
    bb_im = cf_re[..., None] * b_im[l] + cf_im[..., None] * b_re[l]
    gpb = LANES // C
    nblk = G // gpb

    def in_map(bb):
        t = jnp.swapaxes(bb, 1, 2).reshape(nblk, gpb, C, P)
        return jnp.stack([_block_diag(t[a]) for a in range(nblk)]).astype(BF16)

    def out_map(cc):
        t = jnp.swapaxes(cc, 1, 2).reshape(nblk, gpb, P, C)
        return jnp.stack([_block_diag(t[a]) for a in range(nblk)]).astype(BF16)

    def powers(ns):
        ns = jnp.asarray(ns, F32)[:, None, None]
        m = jnp.exp(are[None] * dt[None] * ns)
        ph = aim[None] * dt[None] * ns
        return (m * jnp.cos(ph)).reshape(-1, G * P), (m * jnp.sin(ph)).reshape(-1, G * P)

    p2r, p2i = powers([1.0, 2.0, 4.0])
    p8r, p8i = powers(np.arange(1, SUBLANES + 1, dtype=np.float32))
    pad = lambda t: jnp.pad(t, ((0, SUBLANES - t.shape[0]), (0, 0)))
    return dict(ab_re=ab_re.reshape(1, G * P), ab_im=ab_im.reshape(1, G * P),
                wb_re=in_map(bb_re), wb_im=in_map(bb_im),
                wc_re=out_map(c_re[l]), wc_im=out_map(c_im[l]),
                d=d[l].reshape(1, W), w_glu=w_glu[l].astype(BF16), b_glu=b_glu[l].reshape(1, W),
                pow2_re=pad(p2r), pow2_im=pad(p2i), pow8_re=p8r, pow8_im=p8i)


def _mixers_common(x, mods, lw, name):
    sh_m, sc_m = mods[0], mods[1]
    outs = []
    for key in ("w_gates", "w_pa", "w_pb", "w_pc", "w_pd"):
        outs.append(_inproj(x, lw["norm_mix_g"], sc_m, sh_m, lw[key], name + "_" + key))
    return outs


def _finish(x, mods, lw, gates, o_a, o_b, o_c, o_d):
    x = _merge(o_a, o_b, o_c, o_d, gates, lw["w_branch"], lw["w_out"], x, mods[2])
    return _ffn(x, lw["norm_ffn_g"], mods[4], mods[3], mods[5], lw["ffn_w_up"], lw["ffn_w_down"])


def _layer_prompt(x, mods, lw):
    Bn, L, D = x.shape
    W = lw["W"]
    H = W // HEAD_DIM
    gates, pa, pb, pc, pd = _mixers_common(x, mods, lw, "p")
    prev = jnp.concatenate([jnp.zeros((Bn, 1, pa.shape[2]), F32), pa[:, :-1]], axis=1)
    r, lwd, kk, b, v, k, g = _rwkv_prep(pa, prev, lw["rwkv"])
    y, hfin = _rwkv_chunk(r, lwd, kk, b, v, k, jnp.zeros((Bn, H, HEAD_DIM, HEAD_DIM), F32))
    o_a = _rwkv_post(y, r, k, v, g, lw["rwkv"])
    s_fin = jnp.swapaxes(hfin, 2, 3)
    o_b, hre, him = _s5_prompt(pb, lw["s5"])
    cosf, sinf = lw["rope_p"]
    q, kf, kb, vt, kmean = _attn_prep(pc, True, True, lw["moba"], cosf, sinf)
    o_c = _moba_prompt(q, kb, vt, kmean)
    q2, kb2, vt2 = _attn_prep(pd, False, True)
    o_d = _sb_prompt(q2, kb2, vt2)
    x = _finish(x, mods, lw, gates, o_a, o_b, o_c, o_d)
    G = lw["G"]
    st = (kf.reshape(Bn, L, H, HEAD_DIM), pc[:, :, 2 * W:].reshape(Bn, L, H, HEAD_DIM),
          pd[:, :, W:2 * W].reshape(Bn, L, H, HEAD_DIM), pd[:, :, 2 * W:].reshape(Bn, L, H, HEAD_DIM),
          s_fin, pa[:, -1], hre.reshape(Bn, G, -1), him.reshape(Bn, G, -1))
    return x, st


def _layer_sample(x, mods, lw, shift0, s0, h0_re, h0_im, pools, page_table):
    _, DB, D = x.shape
    W = lw["W"]
    H = W // HEAD_DIM
    G = lw["G"]
    gates, pa, pb, pc, pd = _mixers_common(x, mods, lw, "s")
    r, lwd, kk, b, v, k, g = _rwkv_prep(pa, shift0[None], lw["rwkv"])
    y, s_new = _rwkv_step(s0, r[0], lwd[0], kk[0], b[0], v[0], k[0])
    o_a = _rwkv_post(y[None], r, k, v, g, lw["rwkv"])
    o_b, hre, him = _s5_step(pb[0], h0_re.reshape(DB, -1), h0_im.reshape(DB, -1), lw["s5"])
    pool_mk, pool_mv, pool_sk, pool_sv = pools
    page = pool_mk.shape[1]
    ppb = MOBA_BLOCK // page
    cosf, sinf = lw["rope_s"]
    q, kf = _attn_prep(pc, True, False, lw["moba"], cosf, sinf)
    ksums = _page_sums(pool_mk, page_table)
    idx = _moba_sample_select(ksums, q[0], lw["seg"], ppb)
    sel = jnp.swapaxes(idx[:, :MOBA_TOPK, :H], 1, 2)
    vnew = pc[0, :, 2 * W:]
    o_c = _moba_sample_attn(sel, page_table, pool_mk, pool_mv, q[0], kf[0], vnew, ppb)
    o_d = _sb_sample(pool_sk, pool_sv, page_table, pd[0, :, :W], lw["seg"])
    x = _finish(x, mods, lw, gates, o_a, o_b[None], o_c[None], o_d[None])
    hd = lambda t: t.reshape(DB, 1, H, HEAD_DIM)
    st = (hd(kf[0]), hd(vnew), hd(pd[0, :, W:2 * W]), hd(pd[0, :, 2 * W:]),
          s_new, pa[0], hre.reshape(DB, G, -1), him.reshape(DB, G, -1))
    return x, st


def kernel(x_prompt, x_sample, c_prompt, c_sample, cache_moba_k, cache_moba_v, cache_sb_k, cache_sb_v, state_rwkv, state_rwkv_shift, state_s5_re, state_s5_im, page_table, norm_mix_g, norm_ffn_g, ada_w, ada_b, w_in, rwkv_mu, rwkv_w0, rwkv_w2, rwkv_a0, rwkv_a2, rwkv_g2, rwkv_k_k, rwkv_k_a, rwkv_r_k, rwkv_ln_g, rwkv_ln_b, s5_a_re, s5_a_im, s5_b_re, s5_b_im, s5_c_re, s5_c_im, s5_d, s5_log_dt, s5_w_glu, s5_b_glu, moba_q_norm, moba_k_norm, w_branch, w_out, ffn_w_up, ffn_w_down):
    depth = w_in.shape[0]
    BP, LP, D = x_prompt.shape
    DB = x_sample.shape[0]
    assert x_sample.shape[1] == 1
    W = rwkv_w0.shape[1]
    H = W // HEAD_DIM
    G = s5_a_re.shape[1]
    n_pool, page = cache_moba_k.shape[1], cache_moba_k.shape[2]
    n_pages = page_table.shape[1]
    past_len = n_pages * page
    assert past_len % MOBA_BLOCK == 0 and MOBA_BLOCK % page == 0 and past_len // MOBA_BLOCK >= MOBA_TOPK
    rwkv_in = rwkv_mu.shape[1]
    splits = [N_BRANCH * D, N_BRANCH * D + rwkv_in, N_BRANCH * D + rwkv_in + W, N_BRANCH * D + rwkv_in + 4 * W]

    ones_bd = jnp.kron(jnp.eye(H, dtype=F32), jnp.ones((HEAD_DIM, HEAD_DIM), F32)).astype(BF16)
    seg = jnp.pad(jnp.kron(jnp.eye(H, dtype=F32), jnp.ones((HEAD_DIM, 1), F32)),
                  ((0, 0), (0, LANES - H))).astype(BF16)
    rope_p = _rope_tables(jnp.arange(LP, dtype=jnp.int32), W)
    rope_s = _rope_tables(jnp.full((DB,), past_len, jnp.int32), W)
    rows = BP + DB
    rows_pad = -(-rows // SUBLANES) * SUBLANES
    c_all = jnp.pad(jnp.concatenate([c_prompt, c_sample], axis=0), ((0, rows_pad - rows), (0, 0)))

    yp, ys = x_prompt, x_sample.reshape(1, DB, D)
    st_p_all, st_s_all = [], []
    for l in range(depth):
        wl = w_in[l].astype(BF16)
        zpad = jnp.zeros((RWKV_DECAY_RANK, W), F32)
        lw = dict(
            W=W, G=G, norm_mix_g=norm_mix_g[l].reshape(1, D), norm_ffn_g=norm_ffn_g[l].reshape(1, D),
            w_gates=wl[:, :splits[0]], w_pa=wl[:, splits[0]:splits[1]], w_pb=wl[:, splits[1]:splits[2]],
            w_pc=wl[:, splits[2]:splits[3]], w_pd=wl[:, splits[3]:],
            w_branch=w_branch[l].astype(BF16), w_out=w_out[l].astype(BF16),
            ffn_w_up=ffn_w_up[l].astype(BF16), ffn_w_down=ffn_w_down[l].astype(BF16),
            rope_p=rope_p, rope_s=rope_s, seg=seg,
            rwkv=dict(W=W, mu=rwkv_mu[l].reshape(1, -1), w0=rwkv_w0[l].reshape(1, W), a0=rwkv_a0[l].reshape(1, W),
                      k_k=rwkv_k_k[l].reshape(1, W), k_a=rwkv_k_a[l].reshape(1, W),
                      w2pad=jnp.concatenate([rwkv_w2[l], zpad], axis=0).astype(BF16),
                      a2pad=jnp.concatenate([zpad, rwkv_a2[l]], axis=0).astype(BF16),
                      g2=rwkv_g2[l].astype(BF16), ones_bd=ones_bd,
                      ln_g=rwkv_ln_g[l].reshape(1, W), ln_b=rwkv_ln_b[l].reshape(1, W),
                      r_k=rwkv_r_k[l].reshape(1, W)),
            s5=_s5_params(l, s5_a_re, s5_a_im, s5_b_re, s5_b_im, s5_c_re, s5_c_im, s5_d, s5_log_dt,
                          s5_w_glu, s5_b_glu, W),
            moba=dict(q_norm=jnp.tile(moba_q_norm[l], H).reshape(1, W),
                      k_norm=jnp.tile(moba_k_norm[l], H).reshape(1, W), ones_bd=ones_bd),
        )
        mod = _ada(c_all, ada_w[l].astype(BF16), ada_b[l])
        mods_p = [mod[:BP, i * D:(i + 1) * D].reshape(BP, 1, D) for i in range(6)]
        mods_s = [mod[BP:rows, i * D:(i + 1) * D].reshape(1, DB, D) for i in range(6)]
        yp, st_p = _layer_prompt(yp, mods_p, lw)
        pools = tuple(c[l].reshape(n_pool, page, W) for c in (cache_moba_k, cache_moba_v, cache_sb_k, cache_sb_v))
        ys, st_s = _layer_sample(ys, mods_s, lw, state_rwkv_shift[l], state_rwkv[l], state_s5_re[l],
                                 state_s5_im[l], pools, page_table)
        st_p_all.append(st_p)
        st_s_all.append(st_s)
    stk = lambda lst, i: jnp.stack([s[i] for s in lst], axis=0)
    return (yp, ys.reshape(DB, 1, D),
            stk(st_p_all, 0), stk(st_p_all, 1), stk(st_s_all, 0), stk(st_s_all, 1),
            stk(st_p_all, 2), stk(st_p_all, 3), stk(st_s_all, 2), stk(st_s_all, 3),
            stk(st_p_all, 4), stk(st_s_all, 4),
            stk(st_p_all, 5), stk(st_s_all, 5),
            stk(st_p_all, 6), stk(st_p_all, 7), stk(st_s_all, 6), stk(st_s_all, 7))
```

```python
import functools
import math

import jax
import jax.numpy as jnp
import numpy as np
from jax import lax
from jax.experimental import pallas as pl
from jax.experimental.pallas import tpu as pltpu

F32 = jnp.float32
BF16 = jnp.bfloat16

HEAD_DIM = 64
N_BRANCH = 4
RWKV_DECAY_RANK = 64
RWKV_ICL_RANK = 64
RWKV_GATE_RANK = 128
RWKV_GN_EPS = 64e-5
S5_GROUP = 16
S5_STATE = 64
MOBA_BLOCK = 256
MOBA_TOPK = 3
ROT_DIM = HEAD_DIM // 4
ROPE_THETA = 500000.0
RMS_EPS = 1e-6

LANES = 128
SUBLANES = 8
VMEM_LIMIT_BYTES = 56 * 1024 * 1024
RWKV_CHUNK = 64
NEG_INF = float("-inf")


def _cp(sem):
    return pltpu.CompilerParams(dimension_semantics=sem, vmem_limit_bytes=VMEM_LIMIT_BYTES)


def _sigmoid(x):
    return 1.0 / (1.0 + jnp.exp(-x))


def _softplus(x):
    return jnp.maximum(x, 0.0) + jnp.log(1.0 + jnp.exp(-jnp.abs(x)))


def _gelu_tanh(x):
    return 0.5 * x * (1.0 + jnp.tanh(math.sqrt(2.0 / math.pi) * (x + 0.044715 * (x * x * x))))


def _mm(a, b):
    return jnp.dot(a.astype(BF16), b.astype(BF16), preferred_element_type=F32)


def _mm_nt(a, b):
    return lax.dot_general(a.astype(BF16), b.astype(BF16), (((1,), (1,)), ((), ())),
                           preferred_element_type=F32)


def _split(x, parts):
    out = []
    for _ in range(parts - 1):
        hi = x.astype(BF16)
        out.append(hi)
        x = x - hi.astype(F32)
    out.append(x.astype(BF16))
    return out


def _mm_exact_lhs(a_bf16, x, parts):
    acc = None
    for p in _split(x, parts):
        t = jnp.dot(a_bf16, p, preferred_element_type=F32)
        acc = t if acc is None else acc + t
    return acc


def _mm_exact_rhs(x, b_bf16, parts):
    acc = None
    for p in _split(x, parts):
        t = jnp.dot(p, b_bf16, preferred_element_type=F32)
        acc = t if acc is None else acc + t
    return acc


def _mm3(a, b):
    ah, al = _split(a, 2)
    bh, bl = _split(b, 2)
    return (jnp.dot(ah, bh, preferred_element_type=F32) + jnp.dot(ah, bl, preferred_element_type=F32)
            + jnp.dot(al, bh, preferred_element_type=F32))


def _norm_mod(x, g, sc, sh):
    ms = jnp.mean(x * x, axis=-1, keepdims=True)
    y = x * lax.rsqrt(ms + RMS_EPS)
    return (y * g) * (1.0 + sc) + sh


def _mod_spec(R, tm, D, ngrid):
    if ngrid == 3:
        if R == 1:
            return pl.BlockSpec((1, 1, D), lambda b, i, j: (b, 0, 0))
        return pl.BlockSpec((1, tm, D), lambda b, i, j: (b, i, 0))
    if R == 1:
        return pl.BlockSpec((1, 1, D), lambda b, i: (b, 0, 0))
    return pl.BlockSpec((1, tm, D), lambda b, i: (b, i, 0))


def _row_tile(L, pref):
    t = min(L, pref)
    assert L % t == 0, (L, t)
    return t


def _ada_kernel(c_ref, w_ref, b_ref, o_ref):
    c = c_ref[...]
    o_ref[...] = _mm(c * _sigmoid(c), w_ref[...]) + b_ref[...]


def _ada(c, w_bf16, b):
    rows, D = c.shape
    E = w_bf16.shape[1]
    tn = 1536
    assert E % tn == 0
    return pl.pallas_call(
        _ada_kernel, grid=(E // tn,),
        in_specs=[pl.BlockSpec((rows, D), lambda j: (0, 0)),
                  pl.BlockSpec((D, tn), lambda j: (0, j)),
                  pl.BlockSpec((1, tn), lambda j: (0, j))],
        out_specs=pl.BlockSpec((rows, tn), lambda j: (0, j)),
        out_shape=jax.ShapeDtypeStruct((rows, E), F32),
        compiler_params=_cp(("parallel",)), name="ada_mod")(c, w_bf16, b.reshape(1, E))


def _inproj_kernel(x_ref, g_ref, sc_ref, sh_ref, w_ref, o_ref, h_scr):
    @pl.when(pl.program_id(2) == 0)
    def _():
        h_scr[...] = _norm_mod(x_ref[0], g_ref[...], sc_ref[0], sh_ref[0]).astype(BF16)

    o_ref[0] = jnp.dot(h_scr[...], w_ref[...], preferred_element_type=F32)


def _inproj(x, g, sc, sh, w_bf16, name):
    Bn, L, D = x.shape
    E = w_bf16.shape[1]
    tm = _row_tile(L, 1024)
    tn = 256
    assert E % tn == 0
    R = sc.shape[1]
    ms = _mod_spec(R, tm, D, 3)
    return pl.pallas_call(
        _inproj_kernel, grid=(Bn, L // tm, E // tn),
        in_specs=[pl.BlockSpec((1, tm, D), lambda b, i, j: (b, i, 0)),
                  pl.BlockSpec((1, D), lambda b, i, j: (0, 0)),
                  ms, ms,
                  pl.BlockSpec((D, tn), lambda b, i, j: (0, j))],
        out_specs=pl.BlockSpec((1, tm, tn), lambda b, i, j: (b, i, j)),
        out_shape=jax.ShapeDtypeStruct((Bn, L, E), F32),
        scratch_shapes=[pltpu.VMEM((tm, D), BF16)],
        compiler_params=_cp(("parallel", "parallel", "arbitrary")), name=name)(x, g, sc, sh, w_bf16)


def _merge_kernel(oa_ref, ob_ref, oc_ref, od_ref, gate_ref, wb_ref, wo_ref, x_ref, gm_ref, o_ref, *, D):
    mixed = None
    for n, o in enumerate((oa_ref, ob_ref, oc_ref, od_ref)):
        br = _mm(o[0], wb_ref[n])
        t = _sigmoid(gate_ref[0, :, n * D:(n + 1) * D]) * br
        mixed = t if mixed is None else mixed + t
    o_ref[0] = x_ref[0] + gm_ref[0] * _mm(mixed, wo_ref[...])


def _merge(oa, ob, oc, od, gates, wb_bf16, wo_bf16, x, gm):
    Bn, L, D = x.shape
    W = oa.shape[2]
    tm = _row_tile(L, 256)
    R = gm.shape[1]
    osp = pl.BlockSpec((1, tm, W), lambda b, i: (b, i, 0))
    return pl.pallas_call(
        functools.partial(_merge_kernel, D=D), grid=(Bn, L // tm),
        in_specs=[osp, osp, osp, osp,
                  pl.BlockSpec((1, tm, N_BRANCH * D), lambda b, i: (b, i, 0)),
                  pl.BlockSpec((N_BRANCH, W, D), lambda b, i: (0, 0, 0)),
                  pl.BlockSpec((D, D), lambda b, i: (0, 0)),
                  pl.BlockSpec((1, tm, D), lambda b, i: (b, i, 0)),
                  _mod_spec(R, tm, D, 2)],
        out_specs=pl.BlockSpec((1, tm, D), lambda b, i: (b, i, 0)),
        out_shape=jax.ShapeDtypeStruct((Bn, L, D), F32),
        compiler_params=_cp(("parallel", "parallel")), name="branch_merge")(oa, ob, oc, od, gates, wb_bf16, wo_bf16, x, gm)


def _ffn_kernel(x_ref, g_ref, sc_ref, sh_ref, gf_ref, wg_ref, wv_ref, wd_ref, o_ref, h_scr, acc_scr):
    f = pl.program_id(2)

    @pl.when(f == 0)
    def _():
        h_scr[...] = _norm_mod(x_ref[0], g_ref[...], sc_ref[0], sh_ref[0]).astype(BF16)
        acc_scr[...] = jnp.zeros_like(acc_scr)

    h = h_scr[...]
    ug = jnp.dot(h, wg_ref[...], preferred_element_type=F32)
    uv = jnp.dot(h, wv_ref[...], preferred_element_type=F32)
    act = (ug * _sigmoid(ug)) * uv
    acc_scr[...] += _mm(act, wd_ref[...])

    @pl.when(f == pl.num_programs(2) - 1)
    def _():
        o_ref[0] = x_ref[0] + gf_ref[0] * acc_scr[...]


def _ffn(x, g, sc, sh, gf, wup_bf16, wdown_bf16):
    Bn, L, D = x.shape
    Fdim = wdown_bf16.shape[0]
    tm = _row_tile(L, 1024)
    tf = 256
    assert Fdim % tf == 0
    nf = Fdim // tf
    R = sc.shape[1]
    ms = _mod_spec(R, tm, D, 3)
    return pl.pallas_call(
        _ffn_kernel, grid=(Bn, L // tm, nf),
        in_specs=[pl.BlockSpec((1, tm, D), lambda b, i, f: (b, i, 0)),
                  pl.BlockSpec((1, D), lambda b, i, f: (0, 0)),
                  ms, ms, ms,
                  pl.BlockSpec((D, tf), lambda b, i, f: (0, f)),
                  pl.BlockSpec((D, tf), lambda b, i, f: (0, f + nf)),
                  pl.BlockSpec((tf, D), lambda b, i, f: (f, 0))],
        out_specs=pl.BlockSpec((1, tm, D), lambda b, i, f: (b, i, 0)),
        out_shape=jax.ShapeDtypeStruct((Bn, L, D), F32),
        scratch_shapes=[pltpu.VMEM((tm, D), BF16), pltpu.VMEM((tm, D), F32)],
        compiler_params=_cp(("parallel", "parallel", "arbitrary")), name="ffn")(
            x, g, sc, sh, gf, wup_bf16, wup_bf16, wdown_bf16)


def _segsum(x, ones_bd):
    return _mm_exact_rhs(x, ones_bd, 2)


def _rwkv_prep_kernel(pa_ref, prev_ref, mu_ref, w0_ref, a0_ref, kk_ref, ka_ref, w2_ref, a2_ref, g2_ref, ones_ref,
                      r_o, lw_o, kk_o, b_o, v_o, k_o, g_o, *, W):
    pa = pa_ref[0]
    xs = pa + (prev_ref[0] - pa) * mu_ref[...]
    r = xs[:, 0:W]
    k = xs[:, W:2 * W]
    v = xs[:, 2 * W:3 * W]
    wa = xs[:, 3 * W:3 * W + RWKV_DECAY_RANK + RWKV_ICL_RANK]
    gl = xs[:, 3 * W + RWKV_DECAY_RANK + RWKV_ICL_RANK:]
    w_log = -_softplus(-(w0_ref[...] + _mm(jnp.tanh(wa), w2_ref[...]))) - 0.5
    a = _sigmoid(a0_ref[...] + _mm(wa, a2_ref[...]))
    g = _mm(_sigmoid(gl), g2_ref[...])
    kkraw = k * kk_ref[...]
    n2 = _segsum(kkraw * kkraw, ones_ref[...])
    kk = kkraw / jnp.maximum(jnp.sqrt(n2), 1e-12)
    r_o[0] = r
    lw_o[0] = -jnp.exp(w_log)
    kk_o[0] = kk
    b_o[0] = kk * a
    v_o[0] = v
    k_o[0] = k * (1.0 + (a - 1.0) * ka_ref[...])
    g_o[0] = g


def _rwkv_prep(pa, prev, p):
    Bn, L, RIN = pa.shape
    W = p["W"]
    tm = _row_tile(L, 512)
    row = pl.BlockSpec((1, tm, RIN), lambda b, i: (b, i, 0))
    vec = lambda n: pl.BlockSpec((1, n), lambda b, i: (0, 0))
    mat = lambda s: pl.BlockSpec(s, lambda b, i: (0, 0))
    ospec = pl.BlockSpec((1, tm, W), lambda b, i: (b, i, 0))
    oshape = jax.ShapeDtypeStruct((Bn, L, W), F32)
    return pl.pallas_call(
        functools.partial(_rwkv_prep_kernel, W=W), grid=(Bn, L // tm),
        in_specs=[row, row, vec(RIN), vec(W), vec(W), vec(W), vec(W),
                  mat(p["w2pad"].shape), mat(p["a2pad"].shape), mat(p["g2"].shape), mat((W, W))],
        out_specs=[ospec] * 7, out_shape=[oshape] * 7,
        compiler_params=_cp(("parallel", "parallel")), name="rwkv_prep")(
            pa, prev, p["mu"], p["w0"], p["a0"], p["k_k"], p["k_a"], p["w2pad"], p["a2pad"], p["g2"], p["ones_bd"])


def _rwkv_chunk_kernel(r_ref, lw_ref, kk_ref, b_ref, v_ref, k_ref, h0_ref, y_ref, h_ref, *, T, H):
    @pl.when(pl.program_id(1) == 0)
    def _():
        h_ref[...] = h0_ref[...]

    row = lax.broadcasted_iota(jnp.int32, (T, T), 0)
    col = lax.broadcasted_iota(jnp.int32, (T, T), 1)
    strict = col < row
    incl = col <= row
    blk16 = (row // 16) == (col // 16)
    blk32 = (row // 32) == (col // 32)
    eye = jnp.where(row == col, 1.0, 0.0).astype(F32)

    lw = lw_ref[0]
    G = _mm_exact_lhs(incl.astype(BF16), lw, 3)
    GT = G[T - 1:T, :]
    eG = jnp.exp(G)
    einv = jnp.exp(-G)
    eTs = jnp.exp(GT - G)
    eGT = jnp.exp(GT)
    rt = r_ref[0] * eG
    kkt = kk_ref[0] * jnp.exp(G - lw)
    bh = b_ref[0] * einv
    kh = k_ref[0] * einv
    bT = b_ref[0] * eTs
    kT = k_ref[0] * eTs
    v = v_ref[0]

    hs = range(H)
    sls = [slice(HEAD_DIM * h, HEAD_DIM * (h + 1)) for h in hs]
    per_head = lambda f: [f(h) for h in hs]
    kkt_h = per_head(lambda h: kkt[:, sls[h]])
    rt_h = per_head(lambda h: rt[:, sls[h]])
    v_h = per_head(lambda h: v[:, sls[h]])
    x2 = per_head(lambda h: jnp.concatenate([kkt_h[h], rt_h[h]], axis=0))
    m_b = per_head(lambda h: _mm_nt(x2[h], bh[:, sls[h]]))
    m_k = per_head(lambda h: _mm_nt(x2[h], kh[:, sls[h]]))
    a_ab = per_head(lambda h: jnp.where(strict, m_b[h][:T], 0.0))
    a_ak = per_head(lambda h: jnp.where(strict, m_k[h][:T], 0.0))
    a_rb = per_head(lambda h: jnp.where(incl, m_b[h][T:], 0.0))
    a_rk = per_head(lambda h: jnp.where(incl, m_k[h][T:], 0.0))
    n1 = per_head(lambda h: jnp.where(blk16, -a_ab[h], 0.0))
    inv = per_head(lambda h: eye + n1[h])
    nk = n1
    for _ in range(3):
        nk = per_head(lambda h: _mm(nk[h], nk[h]))
        inv = per_head(lambda h: inv[h] + _mm(inv[h], nk[h]))
    off32 = per_head(lambda h: jnp.where(jnp.logical_and(blk32, jnp.logical_not(blk16)), a_ab[h], 0.0))
    t32 = per_head(lambda h: _mm(inv[h], off32[h]))
    inv = per_head(lambda h: inv[h] - _mm(t32[h], inv[h]))
    off64 = per_head(lambda h: jnp.where(blk32, 0.0, a_ab[h]))
    t64 = per_head(lambda h: _mm(inv[h], off64[h]))
    inv = per_head(lambda h: inv[h] - _mm(t64[h], inv[h]))
    akv = per_head(lambda h: _mm(a_ak[h], v_h[h]))
    yv = per_head(lambda h: _mm(a_rk[h], v_h[h]))
    bTt = per_head(lambda h: bT[:, sls[h]].T)
    hv = per_head(lambda h: _mm(kT[:, sls[h]].T, v_h[h]))
    decay_col = per_head(lambda h: jnp.broadcast_to(eGT[:, sls[h]], (HEAD_DIM, HEAD_DIM)).T)

    hh = per_head(lambda h: h_ref[0, h])
    x2h = per_head(lambda h: _mm(x2[h], hh[h]))
    u = per_head(lambda h: -_mm(inv[h], x2h[h][:T] + akv[h]))
    ys = per_head(lambda h: x2h[h][T:] + _mm(a_rb[h], u[h]) + yv[h])
    hn = per_head(lambda h: decay_col[h] * hh[h] + _mm(bTt[h], u[h]) + hv[h])
    for h in hs:
        h_ref[0, h] = hn[h]
    y_ref[0] = jnp.concatenate(ys, axis=1)


def _rwkv_chunk(r, lw, kk, b, v, k, h0):
    Bn, L, W = r.shape
    H = W // HEAD_DIM
    T = RWKV_CHUNK
    assert L % T == 0
    row = pl.BlockSpec((1, T, W), lambda bi, c: (bi, c, 0))
    st = pl.BlockSpec((1, H, HEAD_DIM, HEAD_DIM), lambda bi, c: (bi, 0, 0, 0))
    return pl.pallas_call(
        functools.partial(_rwkv_chunk_kernel, T=T, H=H), grid=(Bn, L // T),
        in_specs=[row] * 6 + [st],
        out_specs=[row, st],
        out_shape=[jax.ShapeDtypeStruct((Bn, L, W), F32),
                   jax.ShapeDtypeStruct((Bn, H, HEAD_DIM, HEAD_DIM), F32)],
        compiler_params=_cp(("parallel", "arbitrary")), name="rwkv_chunk")(r, lw, kk, b, v, k, h0)


def _rwkv_step_kernel(s_ref, r_ref, lw_ref, kk_ref, b_ref, v_ref, k_ref, y_ref, so_ref):
    S = s_ref[0]
    r, kk, b, v, k = r_ref[0], kk_ref[0], b_ref[0], v_ref[0], k_ref[0]
    w = jnp.exp(lw_ref[0])
    i0 = lax.broadcasted_iota(jnp.int32, (HEAD_DIM, HEAD_DIM), 0)
    i1 = lax.broadcasted_iota(jnp.int32, (HEAD_DIM, HEAD_DIM), 1)
    eye = jnp.where(i0 == i1, 1.0, 0.0).astype(F32)
    sa = jnp.sum(S * kk, axis=-1, keepdims=True)
    vcol = jnp.sum(eye * v, axis=-1, keepdims=True)
    Sn = S * w - sa * b + vcol * k
    ycol = jnp.sum(Sn * r, axis=-1, keepdims=True)
    y_ref[0] = jnp.sum(eye * ycol, axis=-2, keepdims=True)
    so_ref[0] = Sn


def _rwkv_step(S0, r, lw, kk, b, v, k):
    Bn, H = S0.shape[0], S0.shape[1]
    hv = lambda t: t.reshape(Bn, H, 1, HEAD_DIM)
    vs = pl.BlockSpec((1, H, 1, HEAD_DIM), lambda bi: (bi, 0, 0, 0))
    ss = pl.BlockSpec((1, H, HEAD_DIM, HEAD_DIM), lambda bi: (bi, 0, 0, 0))
    y, Sn = pl.pallas_call(
        _rwkv_step_kernel, grid=(Bn,),
        in_specs=[ss] + [vs] * 6, out_specs=[vs, ss],
        out_shape=[jax.ShapeDtypeStruct((Bn, H, 1, HEAD_DIM), F32),
                   jax.ShapeDtypeStruct((Bn, H, HEAD_DIM, HEAD_DIM), F32)],
        compiler_params=_cp(("parallel",)), name="rwkv_step")(S0, hv(r), hv(lw), hv(kk), hv(b), hv(v), hv(k))
    return y.reshape(Bn, H * HEAD_DIM), Sn


def _rwkv_post_kernel(y_ref, r_ref, k_ref, v_ref, g_ref, lng_ref, lnb_ref, rk_ref, ones_ref, o_ref):
    ones = ones_ref[...]
    y = y_ref[0]
    inv_n = 1.0 / HEAD_DIM
    mu = _segsum(y, ones) * inv_n
    d = y - mu
    var = _segsum(d * d, ones) * inv_n
    yn = d * lax.rsqrt(var + RWKV_GN_EPS) * lng_ref[...] + lnb_ref[...]
    bonus = _segsum(r_ref[0] * k_ref[0] * rk_ref[...], ones) * v_ref[0]
    o_ref[0] = (yn + bonus) * g_ref[0]


def _rwkv_post(y, r, k, v, g, p):
    Bn, L, W = y.shape
    tm = _row_tile(L, 512)
    row = pl.BlockSpec((1, tm, W), lambda b, i: (b, i, 0))
    vec = pl.BlockSpec((1, W), lambda b, i: (0, 0))
    return pl.pallas_call(
        _rwkv_post_kernel, grid=(Bn, L // tm),
        in_specs=[row] * 5 + [vec] * 3 + [pl.BlockSpec((W, W), lambda b, i: (0, 0))],
        out_specs=row, out_shape=jax.ShapeDtypeStruct((Bn, L, W), F32),
        compiler_params=_cp(("parallel", "parallel")), name="rwkv_post")(
            y, r, k, v, g, p["ln_g"], p["ln_b"], p["r_k"], p["ones_bd"])


def _s5_in(u, wb_re_ref, wb_im_ref, nblk):
    ub = u.astype(BF16)
    res_re, res_im = [], []
    for a in range(nblk):
        ua = ub[:, LANES * a:LANES * (a + 1)]
        res_re.append(jnp.dot(ua, wb_re_ref[a], preferred_element_type=F32))
        res_im.append(jnp.dot(ua, wb_im_ref[a], preferred_element_type=F32))
    return jnp.concatenate(res_re, axis=1), jnp.concatenate(res_im, axis=1)


def _s5_out(h_re, h_im, u, wc_re_ref, wc_im_ref, d_ref, wglu_ref, bglu_ref, nblk):
    hr = h_re.astype(BF16)
    hi = h_im.astype(BF16)
    spb = hr.shape[1] // nblk
    ys = []
    for a in range(nblk):
        sl = slice(spb * a, spb * (a + 1))
        ys.append(jnp.dot(hr[:, sl], wc_re_ref[a], preferred_element_type=F32)
                  - jnp.dot(hi[:, sl], wc_im_ref[a], preferred_element_type=F32))
    y = jnp.concatenate(ys, axis=1) + d_ref[...] * u
    yg = _gelu_tanh(y)
    return yg * _sigmoid(_mm(yg, wglu_ref[...]) + bglu_ref[...])


def _s5_scan_kernel(u_ref, h0re_ref, h0im_ref, wbre_ref, wbim_ref, wcre_ref, wcim_ref, d_ref, wglu_ref, bglu_ref,
                    p2re_ref, p2im_ref, p8re_ref, p8im_ref, o_ref, hre_o, him_o, *, T, nblk):
    @pl.when(pl.program_id(1) == 0)
    def _():
        hre_o[0] = h0re_ref[0]
        him_o[0] = h0im_ref[0]

    u = u_ref[0]
    bre, bim = _s5_in(u, wbre_ref, wbim_ref, nblk)
    NS = bre.shape[1]
    ng = T // SUBLANES
    xr = bre.reshape(ng, SUBLANES, NS)
    xi = bim.reshape(ng, SUBLANES, NS)
    sub = lax.broadcasted_iota(jnp.int32, (1, SUBLANES, 1), 1)
    for s, d in enumerate((1, 2, 4)):
        ar = p2re_ref[s:s + 1, :].reshape(1, 1, NS)
        ai = p2im_ref[s:s + 1, :].reshape(1, 1, NS)
        keep = sub >= d
        sr = jnp.where(keep, pltpu.roll(xr, d, 1), 0.0)
        si = jnp.where(keep, pltpu.roll(xi, d, 1), 0.0)
        xr, xi = xr + ar * sr - ai * si, xi + ar * si + ai * sr
    p8r = p8re_ref[...]
    p8i = p8im_ref[...]
    cr = hre_o[0]
    ci = him_o[0]
    outs_r, outs_i = [], []
    for gidx in range(ng):
        hr = xr[gidx] + p8r * cr - p8i * ci
        hi = xi[gidx] + p8r * ci + p8i * cr
        outs_r.append(hr)
        outs_i.append(hi)
        cr = hr[SUBLANES - 1:SUBLANES, :]
        ci = hi[SUBLANES - 1:SUBLANES, :]
    hre_o[0] = cr
    him_o[0] = ci
    h_re = jnp.concatenate(outs_r, axis=0)
    h_im = jnp.concatenate(outs_i, axis=0)
    o_ref[0] = _s5_out(h_re, h_im, u, wcre_ref, wcim_ref, d_ref, wglu_ref, bglu_ref, nblk)


def _s5_prompt(u, p):
    Bn, L, W = u.shape
    NS = p["ab_re"].shape[1]
    T = _row_tile(L, 128)
    nblk = W // LANES
    row = pl.BlockSpec((1, T, W), lambda b, c: (b, c, 0))
    st = pl.BlockSpec((1, 1, NS), lambda b, c: (b, 0, 0))
    cst = lambda a: pl.BlockSpec(a.shape, lambda b, c: (0,) * a.ndim)
    h0 = jnp.zeros((Bn, 1, NS), F32)
    consts = [p[k] for k in ("wb_re", "wb_im", "wc_re", "wc_im", "d", "w_glu", "b_glu",
                             "pow2_re", "pow2_im", "pow8_re", "pow8_im")]
    return pl.pallas_call(
        functools.partial(_s5_scan_kernel, T=T, nblk=nblk), grid=(Bn, L // T),
        in_specs=[row, st, st] + [cst(a) for a in consts],
        out_specs=[row, st, st],
        out_shape=[jax.ShapeDtypeStruct((Bn, L, W), F32),
                   jax.ShapeDtypeStruct((Bn, 1, NS), F32), jax.ShapeDtypeStruct((Bn, 1, NS), F32)],
        compiler_params=_cp(("parallel", "arbitrary")), name="s5_scan")(u, h0, h0, *consts)


def _s5_step_kernel(u_ref, h0re_ref, h0im_ref, abre_ref, abim_ref, wbre_ref, wbim_ref, wcre_ref, wcim_ref,
                    d_ref, wglu_ref, bglu_ref, o_ref, hre_o, him_o, *, nblk):
    u = u_ref[...]
    bre, bim = _s5_in(u, wbre_ref, wbim_ref, nblk)
    ar, ai = abre_ref[...], abim_ref[...]
    r0, i0 = h0re_ref[...], h0im_ref[...]
    h_re = bre + ar * r0 - ai * i0
    h_im = bim + ar * i0 + ai * r0
    hre_o[...] = h_re
    him_o[...] = h_im
    o_ref[...] = _s5_out(h_re, h_im, u, wcre_ref, wcim_ref, d_ref, wglu_ref, bglu_ref, nblk)


def _s5_step(u, h0_re, h0_im, p):
    rows, W = u.shape
    NS = p["ab_re"].shape[1]
    nblk = W // LANES
    consts = [p[k] for k in ("ab_re", "ab_im", "wb_re", "wb_im", "wc_re", "wc_im", "d", "w_glu", "b_glu")]
    args = [u, h0_re, h0_im] + consts
    full = lambda a: pl.BlockSpec(a.shape, lambda i: (0,) * a.ndim)
    return pl.pallas_call(
        functools.partial(_s5_step_kernel, nblk=nblk), grid=(1,),
        in_specs=[full(a) for a in args],
        out_specs=[full(u), full(h0_re), full(h0_im)],
        out_shape=[jax.ShapeDtypeStruct((rows, W), F32), jax.ShapeDtypeStruct((rows, NS), F32),
                   jax.ShapeDtypeStruct((rows, NS), F32)],
        compiler_params=_cp(("arbitrary",)), name="s5_step")(*args)


def _attn_prep_kernel(*refs, normrope, emit_kv, W):
    refs = list(refs)
    q_ref, k_ref, v_ref = refs[:3]
    if normrope:
        qn_ref, kn_ref, cos_ref, sin_ref, ones_ref = refs[3:8]
    outs = refs[8:] if normrope else refs[3:]
    q_o = outs.pop(0)
    if normrope:
        k_o = outs.pop(0)
    if emit_kv:
        kb_o, vt_o = outs[0], outs[1]
        if normrope:
            km_o = outs[2]
    q, k = q_ref[0], k_ref[0]
    if normrope:
        ones = ones_ref[...]
        cosf, sinf = cos_ref[...], sin_ref[...]
        dmod = lax.broadcasted_iota(jnp.int32, (1, W), 1) % HEAD_DIM
        half = ROT_DIM // 2

        def nr(x, g):
            ms = _segsum(x * x, ones) * (1.0 / HEAD_DIM)
            xn = x * lax.rsqrt(ms + RMS_EPS) * g
            up = pltpu.roll(xn, W - half, 1)
            dn = pltpu.roll(xn, half, 1)
            rot = jnp.where(dmod < half, -up, jnp.where(dmod < ROT_DIM, dn, 0.0))
            return xn * cosf + rot * sinf

        q = nr(q, qn_ref[...])
        k = nr(k, kn_ref[...])
        k_o[0] = k
    q_o[0] = q
    if emit_kv:
        kb_o[0] = k.astype(BF16)
        vt_o[0] = v_ref[0].T.astype(BF16)
        if normrope:
            km_o[0, 0] = jnp.mean(k, axis=0, keepdims=True)


def _attn_prep(pq, normrope, emit_kv, p=None, cosf=None, sinf=None):
    Bn, L, W3 = pq.shape
    W = W3 // 3
    tm = _row_tile(L, MOBA_BLOCK)
    nbk = L // tm
    col = lambda c: pl.BlockSpec((1, tm, W), lambda b, i, c=c: (b, i, c))
    row_o = pl.BlockSpec((1, tm, W), lambda b, i: (b, i, 0))
    vt_o = pl.BlockSpec((1, W, tm), lambda b, i: (b, 0, i))
    in_specs = [col(0), col(1), col(2)]
    args = [pq, pq, pq]
    out_specs = [row_o]
    out_shape = [jax.ShapeDtypeStruct((Bn, L, W), F32)]
    if normrope:
        vec = pl.BlockSpec((1, W), lambda b, i: (0, 0))
        tab = pl.BlockSpec((tm, W), lambda b, i: (i, 0))
        in_specs += [vec, vec, tab, tab, pl.BlockSpec((W, W), lambda b, i: (0, 0))]
        args += [p["q_norm"], p["k_norm"], cosf, sinf, p["ones_bd"]]
        out_specs.append(row_o)
        out_shape.append(jax.ShapeDtypeStruct((Bn, L, W), F32))
    if emit_kv:
        out_specs += [row_o, vt_o]
        out_shape += [jax.ShapeDtypeStruct((Bn, L, W), BF16), jax.ShapeDtypeStruct((Bn, W, L), BF16)]
        if normrope:
            out_specs.append(pl.BlockSpec((1, 1, 1, W), lambda b, i: (b, i, 0, 0)))
            out_shape.append(jax.ShapeDtypeStruct((Bn, nbk, 1, W), F32))
    return pl.pallas_call(
        functools.partial(_attn_prep_kernel, normrope=normrope, emit_kv=emit_kv, W=W), grid=(Bn, nbk),
        in_specs=in_specs, out_specs=out_specs, out_shape=out_shape,
        compiler_params=_cp(("parallel", "parallel")),
        name="moba_prep" if normrope else "sb_prep")(*args)


def _moba_kernel(q_ref, kb_ref, vt_ref, km_ref, o_ref, sel_scr, *, BQ, NB):
    i = pl.program_id(2)
    qT = q_ref[0].T
    hrow = lax.broadcasted_iota(jnp.int32, (LANES, 1), 0) // HEAD_DIM
    km = km_ref[0]
    nidx = lax.broadcasted_iota(jnp.int32, (NB, 1), 0)
    kpos = lax.broadcasted_iota(jnp.int32, (BQ, BQ), 0)
    qpos = lax.broadcasted_iota(jnp.int32, (BQ, BQ), 1)
    scale = HEAD_DIM ** -0.5
    qfull = []
    for hh in range(2):
        qm = jnp.where(hrow == hh, qT, 0.0)
        gate = _mm3(km, qm)
        valid = nidx < i
        g1 = jnp.where(valid, gate, NEG_INF)
        m1 = jnp.max(g1, axis=0, keepdims=True)
        g2 = jnp.where(g1 >= m1, NEG_INF, g1)
        m2 = jnp.max(g2, axis=0, keepdims=True)
        g3 = jnp.where(g2 >= m2, NEG_INF, g2)
        m3 = jnp.max(g3, axis=0, keepdims=True)
        sel_scr[hh] = jnp.where(jnp.logical_and(valid, gate >= m3), 1.0, 0.0)
        qfull.append((qm * scale).astype(BF16))

    nq = BQ // LANES
    chains = [(hh, c) for hh in range(2) for c in range(nq)]
    lsl = [slice(LANES * c, LANES * (c + 1)) for _, c in chains]
    qms = [qfull[hh][:, lsl[n]] for n, (hh, _) in enumerate(chains)]
    each = lambda f: [f(n) for n in range(len(chains))]

    kj = kb_ref[0, pl.ds(i * BQ, BQ), :]
    vj = vt_ref[0, :, pl.ds(i * BQ, BQ)]
    causal = kpos <= qpos
    s = each(lambda n: jnp.where(causal[:, lsl[n]], jnp.dot(kj, qms[n], preferred_element_type=F32), NEG_INF))
    m0 = each(lambda n: jnp.max(s[n], axis=0, keepdims=True))
    p0 = each(lambda n: jnp.exp(s[n] - m0[n]))
    l0 = each(lambda n: jnp.sum(p0[n], axis=0, keepdims=True))
    a0 = each(lambda n: jnp.dot(vj, p0[n].astype(BF16), preferred_element_type=F32))

    def body(j, st):
        m_old, l_old, a_old = st
        kj = kb_ref[0, pl.ds(j * BQ, BQ), :]
        vj = vt_ref[0, :, pl.ds(j * BQ, BQ)]
        selrow = [sel_scr[hh, pl.ds(j, 1), :] > 0.5 for hh in range(2)]
        sel = each(lambda n: selrow[chains[n][0]][:, lsl[n]])
        s = each(lambda n: jnp.where(sel[n], jnp.dot(kj, qms[n], preferred_element_type=F32), NEG_INF))
        m_new = each(lambda n: jnp.maximum(m_old[n], jnp.max(s[n], axis=0, keepdims=True)))
        alpha = each(lambda n: jnp.exp(m_old[n] - m_new[n]))
        pexp = each(lambda n: jnp.exp(s[n] - m_new[n]))
        l_new = each(lambda n: alpha[n] * l_old[n] + jnp.sum(pexp[n], axis=0, keepdims=True))
        a_new = each(lambda n: alpha[n] * a_old[n]
                     + jnp.dot(vj, pexp[n].astype(BF16), preferred_element_type=F32))
        return tuple(m_new), tuple(l_new), tuple(a_new)

    _, l_fin, a_fin = lax.fori_loop(0, i, body, (tuple(m0), tuple(l0), tuple(a0)))
    outs = [jnp.concatenate([a_fin[hh * nq + c] / l_fin[hh * nq + c] for c in range(nq)], axis=1)
            for hh in range(2)]
    o_ref[0] = jnp.where(hrow == 0, outs[0], outs[1]).T


def _moba_prompt(q, kb, vt, kmean):
    Bn, L, W = q.shape
    BQ = MOBA_BLOCK
    assert L % BQ == 0
    NB = L // BQ
    NBP = -(-NB // SUBLANES) * SUBLANES
    km = kmean.reshape(Bn, NB, W)
    if NBP != NB:
        km = jnp.pad(km, ((0, 0), (0, NBP - NB), (0, 0)))
    npair = W // LANES
    return pl.pallas_call(
        functools.partial(_moba_kernel, BQ=BQ, NB=NBP), grid=(Bn, npair, NB),
        in_specs=[pl.BlockSpec((1, BQ, LANES), lambda b, p, i: (b, i, p)),
                  pl.BlockSpec((1, L, LANES), lambda b, p, i: (b, 0, p)),
                  pl.BlockSpec((1, LANES, L), lambda b, p, i: (b, p, 0)),
                  pl.BlockSpec((1, NBP, LANES), lambda b, p, i: (b, 0, p))],
        out_specs=pl.BlockSpec((1, BQ, LANES), lambda b, p, i: (b, i, p)),
        out_shape=jax.ShapeDtypeStruct((Bn, L, W), F32),
        scratch_shapes=[pltpu.VMEM((2, NBP, BQ), F32)],
        compiler_params=_cp(("parallel", "parallel", "arbitrary")), name="moba_attn")(q, kb, vt, km)


def _sb_terms(z):
    sp = jnp.log(1.0 + jnp.exp(-jnp.abs(z)))
    return jnp.minimum(z, 0.0) - sp, jnp.minimum(-z, 0.0) - sp


def _sb_kernel(q_ref, kb_ref, vt_ref, o_ref, *, BQ):
    i = pl.program_id(2)
    qT = q_ref[0].T
    hrow = lax.broadcasted_iota(jnp.int32, (LANES, 1), 0) // HEAD_DIM
    kpos = lax.broadcasted_iota(jnp.int32, (BQ, BQ), 0)
    qpos = lax.broadcasted_iota(jnp.int32, (BQ, BQ), 1)
    later = (qpos > kpos).astype(BF16)
    scale = HEAD_DIM ** -0.5
    nq = BQ // LANES
    chains = [(hh, c) for hh in range(2) for c in range(nq)]
    qfull = [(jnp.where(hrow == hh, qT, 0.0) * scale).astype(BF16) for hh in range(2)]
    qms = [qfull[hh][:, LANES * c:LANES * (c + 1)] for hh, c in chains]
    oks = [(kpos < qpos)[:, LANES * c:LANES * (c + 1)] for _, c in chains]
    each = lambda f: [f(n) for n in range(len(chains))]

    def block(j, diag, carry, acc):
        kj = kb_ref[0, pl.ds(j * BQ, BQ), :]
        vj = vt_ref[0, :, pl.ds(j * BQ, BQ)]
        z = each(lambda n: jnp.dot(kj, qms[n], preferred_element_type=F32))
        terms = each(lambda n: _sb_terms(z[n]))
        ls = each(lambda n: terms[n][0])
        lneg = each(lambda n: jnp.where(oks[n], terms[n][1], 0.0) if diag else terms[n][1])
        between = each(lambda n: _mm_exact_lhs(later, lneg[n], 2) + carry[n])
        wgt = each(lambda n: jnp.exp(ls[n] + between[n]))
        if diag:
            wgt = each(lambda n: jnp.where(oks[n], wgt[n], 0.0))
        carry = each(lambda n: carry[n] + jnp.sum(lneg[n], axis=0, keepdims=True))
        acc = each(lambda n: acc[n] + jnp.dot(vj, wgt[n].astype(BF16), preferred_element_type=F32))
        return carry, acc

    carry0 = each(lambda n: jnp.zeros((1, LANES), F32))
    acc0 = each(lambda n: jnp.zeros((LANES, LANES), F32))
    state = block(i, True, carry0, acc0)

    def body(t, st):
        carry, acc = block(i - 1 - t, False, list(st[0]), list(st[1]))
        return tuple(carry), tuple(acc)

    _, acc = lax.fori_loop(0, i, body, (tuple(state[0]), tuple(state[1])))
    outs = [jnp.concatenate([acc[hh * nq + c] for c in range(nq)], axis=1) for hh in range(2)]
    o_ref[0] = jnp.where(hrow == 0, outs[0], outs[1]).T


def _sb_prompt(q, kb, vt):
    Bn, L, W = q.shape
    BQ = _row_tile(L, MOBA_BLOCK)
    npair = W // LANES
    return pl.pallas_call(
        functools.partial(_sb_kernel, BQ=BQ), grid=(Bn, npair, L // BQ),
        in_specs=[pl.BlockSpec((1, BQ, LANES), lambda b, p, i: (b, i, p)),
                  pl.BlockSpec((1, L, LANES), lambda b, p, i: (b, 0, p)),
                  pl.BlockSpec((1, LANES, L), lambda b, p, i: (b, p, 0))],
        out_specs=pl.BlockSpec((1, BQ, LANES), lambda b, p, i: (b, i, p)),
        out_shape=jax.ShapeDtypeStruct((Bn, L, W), F32),
        compiler_params=_cp(("parallel", "parallel", "arbitrary")), name="sb_attn")(q, kb, vt)


PAGES_PER_STEP = 8


def _pool_view(cache):
    return jnp.transpose(cache, (0, 1, 3, 4, 2))


def _col_bcast(x, H, page):
    DB = x.shape[0]
    return jnp.broadcast_to(x.reshape(DB, H, HEAD_DIM, 1), (DB, H, HEAD_DIM, page))


def _page_specs(l, n, H, page, nsteps, order_desc):
    specs = []
    for t in range(n):
        if order_desc:
            imap = lambda b, s, pt, t=t: (l, pt[b, (nsteps - 1 - s) * n + t], 0, 0, 0)
        else:
            imap = lambda b, s, pt, t=t: (l, pt[b, s * n + t], 0, 0, 0)
        specs.append(pl.BlockSpec((1, 1, H, HEAD_DIM, page), imap))
    return specs


def _moba_gate_kernel(pt_ref, *refs, n, n_pages, ppb):
    k_refs = refs[:n]
    qc_ref, idx_ref, g_scr = refs[n:]
    s = pl.program_id(1)
    qc = qc_ref[0]
    lane = lax.broadcasted_iota(jnp.int32, g_scr.shape, 1)

    @pl.when(s == 0)
    def _():
        g_scr[...] = jnp.zeros_like(g_scr)

    g = g_scr[...]
    for blk in range(n // ppb):
        ksum = k_refs[blk * ppb][0, 0]
        for t in range(1, ppb):
            ksum = ksum + k_refs[blk * ppb + t][0, 0]
        z = jnp.sum(ksum * qc, axis=1)
        g = jnp.where(lane == s * (n // ppb) + blk, jnp.sum(z, axis=-1, keepdims=True), g)
    g_scr[...] = g

    @pl.when(s == pl.num_programs(1) - 1)
    def _():
        NBK = n_pages // ppb
        gate = g * (1.0 / MOBA_BLOCK)
        nidx = lax.broadcasted_iota(jnp.int32, gate.shape, 1)
        olane = lax.broadcasted_iota(jnp.int32, idx_ref.shape[1:], 1)
        out = jnp.zeros(idx_ref.shape[1:], jnp.int32)
        for r in range(MOBA_TOPK):
            m = jnp.max(gate, axis=-1, keepdims=True)
            am = jnp.min(jnp.where(gate >= m, nidx, NBK), axis=-1, keepdims=True)
            out = jnp.where(olane == r, am, out)
            gate = jnp.where(nidx == am, NEG_INF, gate)
        idx_ref[0] = out


def _moba_sample_select(l, pool_k, page_table, q):
    _, NP, H, Dh, page = pool_k.shape
    DB, n_pages = page_table.shape
    n = PAGES_PER_STEP
    ppb = MOBA_BLOCK // page
    assert n_pages % n == 0 and n % ppb == 0
    gs = pltpu.PrefetchScalarGridSpec(
        num_scalar_prefetch=1, grid=(DB, n_pages // n),
        in_specs=_page_specs(l, n, H, page, n_pages // n, False)
        + [pl.BlockSpec((1, H, Dh, page), lambda b, s, pt: (b, 0, 0, 0))],
        out_specs=pl.BlockSpec((1, H, LANES), lambda b, s, pt: (b, 0, 0)),
        scratch_shapes=[pltpu.VMEM((H, n_pages // ppb), F32)])
    return pl.pallas_call(
        functools.partial(_moba_gate_kernel, n=n, n_pages=n_pages, ppb=ppb), grid_spec=gs,
        out_shape=jax.ShapeDtypeStruct((DB, H, LANES), jnp.int32),
        compiler_params=_cp(("parallel", "arbitrary")), name="moba_sample_gate")(
            page_table, *([pool_k] * n), _col_bcast(q, H, page))


def _moba_sample_attn_kernel(sel_ref, pt_ref, *refs, n_pg):
    k_refs, v_refs = refs[:n_pg], refs[n_pg:2 * n_pg]
    qc_ref, kn_ref, vn_ref, o_ref = refs[2 * n_pg:]
    qc = qc_ref[0, 0] * (HEAD_DIM ** -0.5)
    s_new = jnp.sum(qc[:, 0:1] * kn_ref[0, 0], axis=0, keepdims=True)
    ss = [jnp.sum(kr[0, 0, 0] * qc, axis=0, keepdims=True) for kr in k_refs]
    m = s_new
    for s in ss:
        m = jnp.maximum(m, jnp.max(s, axis=-1, keepdims=True))
    p_new = jnp.exp(s_new - m)
    den = p_new
    acc = jnp.zeros(qc.shape, F32)
    for s, vr in zip(ss, v_refs):
        pe = jnp.exp(s - m)
        den = den + jnp.sum(pe, axis=-1, keepdims=True)
        acc = acc + vr[0, 0, 0] * pe
    o_ref[0, 0] = (jnp.sum(acc, axis=-1, keepdims=True) + p_new * vn_ref[0, 0]) / den


def _moba_sample_attn(l, sel, page_table, pool_k, pool_v, q, k_new, v_new):
    DB, H, topk = sel.shape
    _, NP, _, Dh, page = pool_k.shape
    ppb = MOBA_BLOCK // page
    n_pg = topk * ppb

    def pspec(t):
        r, half = divmod(t, ppb)
        return pl.BlockSpec(
            (1, 1, 1, Dh, page),
            lambda b, h, sel_r, pt, r=r, half=half: (l, pt[b, sel_r[b, h, r] * ppb + half], h, 0, 0))

    col = pl.BlockSpec((1, 1, Dh, 1), lambda b, h, sel_r, pt: (b, h, 0, 0))
    in_specs = ([pspec(t) for t in range(n_pg)] * 2
                + [pl.BlockSpec((1, 1, Dh, page), lambda b, h, sel_r, pt: (b, h, 0, 0)), col, col])
    c4 = lambda t: t.reshape(DB, H, Dh, 1)
    gs = pltpu.PrefetchScalarGridSpec(num_scalar_prefetch=2, grid=(DB, H), in_specs=in_specs, out_specs=col)
    out = pl.pallas_call(
        functools.partial(_moba_sample_attn_kernel, n_pg=n_pg), grid_spec=gs,
        out_shape=jax.ShapeDtypeStruct((DB, H, Dh, 1), F32),
        compiler_params=_cp(("parallel", "arbitrary")), name="moba_sample_attn")(
            sel, page_table, *([pool_k] * n_pg), *([pool_v] * n_pg), _col_bcast(q, H, page), c4(k_new), c4(v_new))
    return out.reshape(DB, H * Dh)


def _sb_sample_kernel(pt_ref, *refs, n, page):
    k_refs, v_refs = refs[:n], refs[n:2 * n]
    qc_ref, o_ref, carry_scr, acc_scr = refs[2 * n:]
    s = pl.program_id(1)

    @pl.when(s == 0)
    def _():
        carry_scr[...] = jnp.zeros_like(carry_scr)
        acc_scr[...] = jnp.zeros_like(acc_scr)

    r0 = lax.broadcasted_iota(jnp.int32, (page, page), 0)
    r1 = lax.broadcasted_iota(jnp.int32, (page, page), 1)
    later = (r0 > r1).astype(BF16)
    qc = qc_ref[0] * (HEAD_DIM ** -0.5)
    carry = carry_scr[...]
    acc = acc_scr[...]
    for t in reversed(range(n)):
        z = jnp.sum(k_refs[t][0, 0] * qc, axis=1)
        ls, lneg = _sb_terms(z)
        between = _mm_exact_rhs(lneg, later, 2) + carry
        wgt = jnp.exp(ls + between)
        carry = carry + jnp.sum(lneg, axis=-1, keepdims=True)
        acc = acc + v_refs[t][0, 0] * wgt[:, None, :]
    carry_scr[...] = carry
    acc_scr[...] = acc

    @pl.when(s == pl.num_programs(1) - 1)
    def _():
        o_ref[0] = jnp.sum(acc, axis=-1, keepdims=True)


def _sb_sample(l, pool_k, pool_v, page_table, q):
    _, NP, H, Dh, page = pool_k.shape
    DB, n_pages = page_table.shape
    n = PAGES_PER_STEP
    assert n_pages % n == 0
    nsteps = n_pages // n
    in_specs = (_page_specs(l, n, H, page, nsteps, True) + _page_specs(l, n, H, page, nsteps, True)
                + [pl.BlockSpec((1, H, Dh, page), lambda b, s, pt: (b, 0, 0, 0))])
    gs = pltpu.PrefetchScalarGridSpec(
        num_scalar_prefetch=1, grid=(DB, nsteps), in_specs=in_specs,
        out_specs=pl.BlockSpec((1, H, Dh, 1), lambda b, s, pt: (b, 0, 0, 0)),
        scratch_shapes=[pltpu.VMEM((H, 1), F32), pltpu.VMEM((H, Dh, page), F32)])
    out = pl.pallas_call(
        functools.partial(_sb_sample_kernel, n=n, page=page), grid_spec=gs,
        out_shape=jax.ShapeDtypeStruct((DB, H, Dh, 1), F32),
        compiler_params=_cp(("parallel", "arbitrary")), name="sb_sample")(
            page_table, *([pool_k] * n), *([pool_v] * n), _col_bcast(q, H, page))
    return out.reshape(DB, H * Dh)


def _rope_tables(pos, W):
    half = ROT_DIM // 2
    inv = ROPE_THETA ** (-jnp.arange(half, dtype=F32) * 2.0 / ROT_DIM)
    ang = pos.astype(F32)[:, None] * inv[None, :]
    n = pos.shape[0]
    pad_c = jnp.ones((n, HEAD_DIM - ROT_DIM), F32)
    pad_s = jnp.zeros((n, HEAD_DIM - ROT_DIM), F32)
    cos_h = jnp.concatenate([jnp.cos(ang), jnp.cos(ang), pad_c], axis=1)
    sin_h = jnp.concatenate([jnp.sin(ang), jnp.sin(ang), pad_s], axis=1)
    reps = W // HEAD_DIM
    return jnp.tile(cos_h, (1, reps)), jnp.tile(sin_h, (1, reps))


def _block_diag(blocks):
    G, a, b = blocks.shape
    eye = jnp.eye(G, dtype=blocks.dtype)
    return (blocks[:, :, None, :] * eye[:, None, :, None]).reshape(G * a, G * b)


def _s5_params(l, a_re, a_im, b_re, b_im, c_re, c_im, d, log_dt, w_glu, b_glu, W):
    G, P = a_re.shape[1], a_re.shape[2]
    C = W // G
    are, aim = a_re[l], a_im[l]
    dt = jnp.exp(log_dt[l])[:, None]
    mag = jnp.exp(are * dt)
    ab_re, ab_im = mag * jnp.cos(aim * dt), mag * jnp.sin(aim * dt)
    den = are * are + aim * aim
    n_re = ab_re - 1.0
    cf_re = (n_re * are + ab_im * aim) / den
    cf_im = (ab_im * are - n_re * aim) / den
    bb_re = cf_re[..., None] * b_re[l] - cf_im[..., None] * b_im[l]
    bb_im = cf_re[..., None] * b_im[l] + cf_im[..., None] * b_re[l]
    gpb = LANES // C
    nblk = G // gpb

    def in_map(bb):
        t = jnp.swapaxes(bb, 1, 2).reshape(nblk, gpb, C, P)
        return jnp.stack([_block_diag(t[a]) for a in range(nblk)]).astype(BF16)

    def out_map(cc):
        t = jnp.swapaxes(cc, 1, 2).reshape(nblk, gpb, P, C)
        return jnp.stack([_block_diag(t[a]) for a in range(nblk)]).astype(BF16)

    def powers(ns):
        ns = jnp.asarray(ns, F32)[:, None, None]
        m = jnp.exp(are[None] * dt[None] * ns)
        ph = aim[None] * dt[None] * ns
        return (m * jnp.cos(ph)).reshape(-1, G * P), (m * jnp.sin(ph)).reshape(-1, G * P)

    p2r, p2i = powers([1.0, 2.0, 4.0])
    p8r, p8i = powers(np.arange(1, SUBLANES + 1, dtype=np.float32))
    pad = lambda t: jnp.pad(t, ((0, SUBLANES - t.shape[0]), (0, 0)))
    return dict(ab_re=ab_re.reshape(1, G * P), ab_im=ab_im.reshape(1, G * P),
                wb_re=in_map(bb_re), wb_im=in_map(bb_im),
                wc_re=out_map(c_re[l]), wc_im=out_map(c_im[l]),
                d=d[l].reshape(1, W), w_glu=w_glu[l].astype(BF16), b_glu=b_glu[l].reshape(1, W),
                pow2_re=pad(p2r), pow2_im=pad(p2i), pow8_re=p8r, pow8_im=p8i)


def _mixers_common(x, mods, lw, name):
    sh_m, sc_m = mods[0], mods[1]
    outs = []
    for key in ("w_gates", "w_pa", "w_pb", "w_pc", "w_pd"):
        outs.append(_inproj(x, lw["norm_mix_g"], sc_m, sh_m, lw[key], name + "_" + key))
    return outs


def _finish(x, mods, lw, gates, o_a, o_b, o_c, o_d):
    x = _merge(o_a, o_b, o_c, o_d, gates, lw["w_branch"], lw["w_out"], x, mods[2])
    return _ffn(x, lw["norm_ffn_g"], mods[4], mods[3], mods[5], lw["ffn_w_up"], lw["ffn_w_down"])


def _layer_prompt(x, mods, lw):
    Bn, L, D = x.shape
    W = lw["W"]
    H = W // HEAD_DIM
    gates, pa, pb, pc, pd = _mixers_common(x, mods, lw, "p")
    prev = jnp.concatenate([jnp.zeros((Bn, 1, pa.shape[2]), F32), pa[:, :-1]], axis=1)
    r, lwd, kk, b, v, k, g = _rwkv_prep(pa, prev, lw["rwkv"])
    y, hfin = _rwkv_chunk(r, lwd, kk, b, v, k, jnp.zeros((Bn, H, HEAD_DIM, HEAD_DIM), F32))
    o_a = _rwkv_post(y, r, k, v, g, lw["rwkv"])
    s_fin = jnp.swapaxes(hfin, 2, 3)
    o_b, hre, him = _s5_prompt(pb, lw["s5"])
    cosf, sinf = lw["rope_p"]
    q, kf, kb, vt, kmean = _attn_prep(pc, True, True, lw["moba"], cosf, sinf)
    o_c = _moba_prompt(q, kb, vt, kmean)
    q2, kb2, vt2 = _attn_prep(pd, False, True)
    o_d = _sb_prompt(q2, kb2, vt2)
    x = _finish(x, mods, lw, gates, o_a, o_b, o_c, o_d)
    G = lw["G"]
    st = (kf.reshape(Bn, L, H, HEAD_DIM), pc[:, :, 2 * W:].reshape(Bn, L, H, HEAD_DIM),
          pd[:, :, W:2 * W].reshape(Bn, L, H, HEAD_DIM), pd[:, :, 2 * W:].reshape(Bn, L, H, HEAD_DIM),
          s_fin, pa[:, -1], hre.reshape(Bn, G, -1), him.reshape(Bn, G, -1))
    return x, st


def _layer_sample(l, x, mods, lw, shift0, s0, h0_re, h0_im, pools, page_table):
    _, DB, D = x.shape
    W = lw["W"]
    H = W // HEAD_DIM
    G = lw["G"]
    gates, pa, pb, pc, pd = _mixers_common(x, mods, lw, "s")
    r, lwd, kk, b, v, k, g = _rwkv_prep(pa, shift0[None], lw["rwkv"])
    y, s_new = _rwkv_step(s0, r[0], lwd[0], kk[0], b[0], v[0], k[0])
    o_a = _rwkv_post(y[None], r, k, v, g, lw["rwkv"])
    o_b, hre, him = _s5_step(pb[0], h0_re.reshape(DB, -1), h0_im.reshape(DB, -1), lw["s5"])
    pool_mk, pool_mv, pool_sk, pool_sv = pools
    cosf, sinf = lw["rope_s"]
    q, kf = _attn_prep(pc, True, False, lw["moba"], cosf, sinf)
    sel = _moba_sample_select(l, pool_mk, page_table, q[0])[:, :, :MOBA_TOPK]
    vnew = pc[0, :, 2 * W:]
    o_c = _moba_sample_attn(l, sel, page_table, pool_mk, pool_mv, q[0], kf[0], vnew)
    o_d = _sb_sample(l, pool_sk, pool_sv, page_table, pd[0, :, :W])
    x = _finish(x, mods, lw, gates, o_a, o_b[None], o_c[None], o_d[None])
    hd = lambda t: t.reshape(DB, 1, H, HEAD_DIM)
    st = (hd(kf[0]), hd(vnew), hd(pd[0, :, W:2 * W]), hd(pd[0, :, 2 * W:]),
          s_new, pa[0], hre.reshape(DB, G, -1), him.reshape(DB, G, -1))
    return x, st


def kernel(x_prompt, x_sample, c_prompt, c_sample, cache_moba_k, cache_moba_v, cache_sb_k, cache_sb_v, state_rwkv, state_rwkv_shift, state_s5_re, state_s5_im, page_table, norm_mix_g, norm_ffn_g, ada_w, ada_b, w_in, rwkv_mu, rwkv_w0, rwkv_w2, rwkv_a0, rwkv_a2, rwkv_g2, rwkv_k_k, rwkv_k_a, rwkv_r_k, rwkv_ln_g, rwkv_ln_b, s5_a_re, s5_a_im, s5_b_re, s5_b_im, s5_c_re, s5_c_im, s5_d, s5_log_dt, s5_w_glu, s5_b_glu, moba_q_norm, moba_k_norm, w_branch, w_out, ffn_w_up, ffn_w_down):
    depth = w_in.shape[0]
    BP, LP, D = x_prompt.shape
    DB = x_sample.shape[0]
    assert x_sample.shape[1] == 1
    W = rwkv_w0.shape[1]
    H = W // HEAD_DIM
    G = s5_a_re.shape[1]
    n_pool, page = cache_moba_k.shape[1], cache_moba_k.shape[2]
    n_pages = page_table.shape[1]
    past_len = n_pages * page
    assert past_len % MOBA_BLOCK == 0 and MOBA_BLOCK % page == 0 and past_len // MOBA_BLOCK >= MOBA_TOPK
    rwkv_in = rwkv_mu.shape[1]
    splits = [N_BRANCH * D, N_BRANCH * D + rwkv_in, N_BRANCH * D + rwkv_in + W, N_BRANCH * D + rwkv_in + 4 * W]

    ones_bd = jnp.kron(jnp.eye(H, dtype=F32), jnp.ones((HEAD_DIM, HEAD_DIM), F32)).astype(BF16)
    pools = tuple(_pool_view(c) for c in (cache_moba_k, cache_moba_v, cache_sb_k, cache_sb_v))
    rope_p = _rope_tables(jnp.arange(LP, dtype=jnp.int32), W)
    rope_s = _rope_tables(jnp.full((DB,), past_len, jnp.int32), W)
    rows = BP + DB
    rows_pad = -(-rows // SUBLANES) * SUBLANES
    c_all = jnp.pad(jnp.concatenate([c_prompt, c_sample], axis=0), ((0, rows_pad - rows), (0, 0)))

    yp, ys = x_prompt, x_sample.reshape(1, DB, D)
    st_p_all, st_s_all = [], []
    for l in range(depth):
        wl = w_in[l].astype(BF16)
        zpad = jnp.zeros((RWKV_DECAY_RANK, W), F32)
        lw = dict(
            W=W, G=G, norm_mix_g=norm_mix_g[l].reshape(1, D), norm_ffn_g=norm_ffn_g[l].reshape(1, D),
            w_gates=wl[:, :splits[0]], w_pa=wl[:, splits[0]:splits[1]], w_pb=wl[:, splits[1]:splits[2]],
            w_pc=wl[:, splits[2]:splits[3]], w_pd=wl[:, splits[3]:],
            w_branch=w_branch[l].astype(BF16), w_out=w_out[l].astype(BF16),
            ffn_w_up=ffn_w_up[l].astype(BF16), ffn_w_down=ffn_w_down[l].astype(BF16),
            rope_p=rope_p, rope_s=rope_s,
            rwkv=dict(W=W, mu=rwkv_mu[l].reshape(1, -1), w0=rwkv_w0[l].reshape(1, W), a0=rwkv_a0[l].reshape(1, W),
                      k_k=rwkv_k_k[l].reshape(1, W), k_a=rwkv_k_a[l].reshape(1, W),
                      w2pad=jnp.concatenate([rwkv_w2[l], zpad], axis=0).astype(BF16),
                      a2pad=jnp.concatenate([zpad, rwkv_a2[l]], axis=0).astype(BF16),
                      g2=rwkv_g2[l].astype(BF16), ones_bd=ones_bd,
                      ln_g=rwkv_ln_g[l].reshape(1, W), ln_b=rwkv_ln_b[l].reshape(1, W),
                      r_k=rwkv_r_k[l].reshape(1, W)),
            s5=_s5_params(l, s5_a_re, s5_a_im, s5_b_re, s5_b_im, s5_c_re, s5_c_im, s5_d, s5_log_dt,
                          s5_w_glu, s5_b_glu, W),
            moba=dict(q_norm=jnp.tile(moba_q_norm[l], H).reshape(1, W),
                      k_norm=jnp.tile(moba_k_norm[l], H).reshape(1, W), ones_bd=ones_bd),
        )
        mod = _ada(c_all, ada_w[l].astype(BF16), ada_b[l])
        mods_p = [mod[:BP, i * D:(i + 1) * D].reshape(BP, 1, D) for i in range(6)]
        mods_s = [mod[BP:rows, i * D:(i + 1) * D].reshape(1, DB, D) for i in range(6)]
        yp, st_p = _layer_prompt(yp, mods_p, lw)
        ys, st_s = _layer_sample(l, ys, mods_s, lw, state_rwkv_shift[l], state_rwkv[l], state_s5_re[l],
                                 state_s5_im[l], pools, page_table)
        st_p_all.append(st_p)
        st_s_all.append(st_s)
    stk = lambda lst, i: jnp.stack([s[i] for s in lst], axis=0)
    return (yp, ys.reshape(DB, 1, D),
            stk(st_p_all, 0), stk(st_p_all, 1), stk(st_s_all, 0), stk(st_s_all, 1),
            stk(st_p_all, 2), stk(st_p_all, 3), stk(st_s_all, 2), stk(st_s_all, 3),
            stk(st_p_all, 4), stk(st_s_all, 4),
            stk(st_p_all, 5), stk(st_s_all, 5),
            stk(st_p_all, 6), stk(st_p_all, 7), stk(st_s_all, 6), stk(st_s_all, 7))
```

```python
import functools
import math

import jax
import jax.numpy as jnp
import numpy as np
from jax import lax
from jax.experimental import pallas as pl
from jax.experimental.pallas import tpu as pltpu

F32 = jnp.float32
BF16 = jnp.bfloat16

HEAD_DIM = 64
N_BRANCH = 4
RWKV_DECAY_RANK = 64
RWKV_ICL_RANK = 64
RWKV_GATE_RANK = 128
RWKV_GN_EPS = 64e-5
S5_GROUP = 16
S5_STATE = 64
MOBA_BLOCK = 256
MOBA_TOPK = 3
ROT_DIM = HEAD_DIM // 4
ROPE_THETA = 500000.0
RMS_EPS = 1e-6

LANES = 128
SUBLANES = 8
VMEM_LIMIT_BYTES = 56 * 1024 * 1024
RWKV_CHUNK = 64
NEG_INF = float("-inf")


def _cp(sem):
    return pltpu.CompilerParams(dimension_semantics=sem, vmem_limit_bytes=VMEM_LIMIT_BYTES)


def _sigmoid(x):
    return 1.0 / (1.0 + jnp.exp(-x))


def _softplus(x):
    return jnp.maximum(x, 0.0) + jnp.log(1.0 + jnp.exp(-jnp.abs(x)))


def _gelu_tanh(x):
    return 0.5 * x * (1.0 + jnp.tanh(math.sqrt(2.0 / math.pi) * (x + 0.044715 * (x * x * x))))


def _mm(a, b):
    return jnp.dot(a.astype(BF16), b.astype(BF16), preferred_element_type=F32)


def _mm_nt(a, b):
    return lax.dot_general(a.astype(BF16), b.astype(BF16), (((1,), (1,)), ((), ())),
                           preferred_element_type=F32)


def _split(x, parts):
    out = []
    for _ in range(parts - 1):
        hi = x.astype(BF16)
        out.append(hi)
        x = x - hi.astype(F32)
    out.append(x.astype(BF16))
    return out


def _mm_exact_lhs(a_bf16, x, parts):
    acc = None
    for p in _split(x, parts):
        t = jnp.dot(a_bf16, p, preferred_element_type=F32)
        acc = t if acc is None else acc + t
    return acc


def _mm_exact_rhs(x, b_bf16, parts):
    acc = None
    for p in _split(x, parts):
        t = jnp.dot(p, b_bf16, preferred_element_type=F32)
        acc = t if acc is None else acc + t
    return acc


def _mm3(a, b):
    ah, al = _split(a, 2)
    bh, bl = _split(b, 2)
    return (jnp.dot(ah, bh, preferred_element_type=F32) + jnp.dot(ah, bl, preferred_element_type=F32)
            + jnp.dot(al, bh, preferred_element_type=F32))


def _norm_mod(x, g, sc, sh):
    ms = jnp.mean(x * x, axis=-1, keepdims=True)
    y = x * lax.rsqrt(ms + RMS_EPS)
    return (y * g) * (1.0 + sc) + sh


def _mod_spec(R, tm, D, ngrid):
    if ngrid == 3:
        if R == 1:
            return pl.BlockSpec((1, 1, D), lambda b, i, j: (b, 0, 0))
        return pl.BlockSpec((1, tm, D), lambda b, i, j: (b, i, 0))
    if R == 1:
        return pl.BlockSpec((1, 1, D), lambda b, i: (b, 0, 0))
    return pl.BlockSpec((1, tm, D), lambda b, i: (b, i, 0))


def _row_tile(L, pref):
    t = min(L, pref)
    assert L % t == 0, (L, t)
    return t


def _ada_kernel(c_ref, w_ref, b_ref, o_ref):
    c = c_ref[...]
    o_ref[...] = _mm(c * _sigmoid(c), w_ref[...]) + b_ref[...]


def _ada(c, w_bf16, b):
    rows, D = c.shape
    E = w_bf16.shape[1]
    tn = 1536
    assert E % tn == 0
    return pl.pallas_call(
        _ada_kernel, grid=(E // tn,),
        in_specs=[pl.BlockSpec((rows, D), lambda j: (0, 0)),
                  pl.BlockSpec((D, tn), lambda j: (0, j)),
                  pl.BlockSpec((1, tn), lambda j: (0, j))],
        out_specs=pl.BlockSpec((rows, tn), lambda j: (0, j)),
        out_shape=jax.ShapeDtypeStruct((rows, E), F32),
        compiler_params=_cp(("parallel",)), name="ada_mod")(c, w_bf16, b.reshape(1, E))


def _inproj_kernel(x_ref, g_ref, sc_ref, sh_ref, w_ref, o_ref, h_scr):
    @pl.when(pl.program_id(2) == 0)
    def _():
        h_scr[...] = _norm_mod(x_ref[0], g_ref[...], sc_ref[0], sh_ref[0]).astype(BF16)

    o_ref[0] = jnp.dot(h_scr[...], w_ref[...], preferred_element_type=F32)


def _inproj(x, g, sc, sh, w_bf16, name):
    Bn, L, D = x.shape
    E = w_bf16.shape[1]
    tm = _row_tile(L, 1024)
    tn = 512
    assert E % tn == 0
    R = sc.shape[1]
    ms = _mod_spec(R, tm, D, 3)
    return pl.pallas_call(
        _inproj_kernel, grid=(Bn, L // tm, E // tn),
        in_specs=[pl.BlockSpec((1, tm, D), lambda b, i, j: (b, i, 0)),
                  pl.BlockSpec((1, D), lambda b, i, j: (0, 0)),
                  ms, ms,
                  pl.BlockSpec((D, tn), lambda b, i, j: (0, j))],
        out_specs=pl.BlockSpec((1, tm, tn), lambda b, i, j: (b, i, j)),
        out_shape=jax.ShapeDtypeStruct((Bn, L, E), F32),
        scratch_shapes=[pltpu.VMEM((tm, D), BF16)],
        compiler_params=_cp(("parallel", "parallel", "arbitrary")), name=name)(x, g, sc, sh, w_bf16)


def _merge_kernel(oa_ref, ob_ref, oc_ref, od_ref, g0_ref, g1_ref, g2_ref, g3_ref, wb_ref, wo_ref, x_ref, gm_ref,
                  o_ref):
    mixed = None
    for n, (o, gt) in enumerate(zip((oa_ref, ob_ref, oc_ref, od_ref), (g0_ref, g1_ref, g2_ref, g3_ref))):
        t = _sigmoid(gt[0]) * _mm(o[0], wb_ref[n])
        mixed = t if mixed is None else mixed + t
    o_ref[0] = x_ref[0] + gm_ref[0] * _mm(mixed, wo_ref[...])


def _merge(oa, ob, oc, od, proj, gate_col, wb_bf16, wo_bf16, x, gm):
    Bn, L, D = x.shape
    W = oa.shape[2]
    tm = _row_tile(L, 256)
    R = gm.shape[1]
    assert gate_col % D == 0
    osp = pl.BlockSpec((1, tm, W), lambda b, i: (b, i, 0))
    gsp = [pl.BlockSpec((1, tm, D), lambda b, i, c=gate_col // D + n: (b, i, c)) for n in range(N_BRANCH)]
    return pl.pallas_call(
        _merge_kernel, grid=(Bn, L // tm),
        in_specs=[osp, osp, osp, osp] + gsp + [
            pl.BlockSpec((N_BRANCH, W, D), lambda b, i: (0, 0, 0)),
            pl.BlockSpec((D, D), lambda b, i: (0, 0)),
            pl.BlockSpec((1, tm, D), lambda b, i: (b, i, 0)),
            _mod_spec(R, tm, D, 2)],
        out_specs=pl.BlockSpec((1, tm, D), lambda b, i: (b, i, 0)),
        out_shape=jax.ShapeDtypeStruct((Bn, L, D), F32),
        compiler_params=_cp(("parallel", "parallel")), name="branch_merge")(
            oa, ob, oc, od, proj, proj, proj, proj, wb_bf16, wo_bf16, x, gm)


def _ffn_kernel(x_ref, g_ref, sc_ref, sh_ref, gf_ref, wg_ref, wv_ref, wd_ref, o_ref, h_scr, acc_scr):
    f = pl.program_id(2)

    @pl.when(f == 0)
    def _():
        h_scr[...] = _norm_mod(x_ref[0], g_ref[...], sc_ref[0], sh_ref[0]).astype(BF16)
        acc_scr[...] = jnp.zeros_like(acc_scr)

    h = h_scr[...]
    ug = jnp.dot(h, wg_ref[...], preferred_element_type=F32)
    uv = jnp.dot(h, wv_ref[...], preferred_element_type=F32)
    act = (ug * _sigmoid(ug)) * uv
    acc_scr[...] += _mm(act, wd_ref[...])

    @pl.when(f == pl.num_programs(2) - 1)
    def _():
        o_ref[0] = x_ref[0] + gf_ref[0] * acc_scr[...]


def _ffn(x, g, sc, sh, gf, wup_bf16, wdown_bf16):
    Bn, L, D = x.shape
    Fdim = wdown_bf16.shape[0]
    tm = _row_tile(L, 1024)
    tf = 256
    assert Fdim % tf == 0
    nf = Fdim // tf
    R = sc.shape[1]
    ms = _mod_spec(R, tm, D, 3)
    return pl.pallas_call(
        _ffn_kernel, grid=(Bn, L // tm, nf),
        in_specs=[pl.BlockSpec((1, tm, D), lambda b, i, f: (b, i, 0)),
                  pl.BlockSpec((1, D), lambda b, i, f: (0, 0)),
                  ms, ms, ms,
                  pl.BlockSpec((D, tf), lambda b, i, f: (0, f)),
                  pl.BlockSpec((D, tf), lambda b, i, f: (0, f + nf)),
                  pl.BlockSpec((tf, D), lambda b, i, f: (f, 0))],
        out_specs=pl.BlockSpec((1, tm, D), lambda b, i, f: (b, i, 0)),
        out_shape=jax.ShapeDtypeStruct((Bn, L, D), F32),
        scratch_shapes=[pltpu.VMEM((tm, D), BF16), pltpu.VMEM((tm, D), F32)],
        compiler_params=_cp(("parallel", "parallel", "arbitrary")), name="ffn")(
            x, g, sc, sh, gf, wup_bf16, wup_bf16, wdown_bf16)


def _segsum(x, ones_bd):
    return _mm_exact_rhs(x, ones_bd, 2)


def _rwkv_prep_kernel(pa_ref, prev_ref, mu_ref, w0_ref, a0_ref, kk_ref, ka_ref, w2_ref, a2_ref, g2_ref, ones_ref,
                      r_o, lw_o, kk_o, b_o, v_o, k_o, g_o, *, W):
    pa = pa_ref[0]
    xs = pa + (prev_ref[0] - pa) * mu_ref[...]
    r = xs[:, 0:W]
    k = xs[:, W:2 * W]
    v = xs[:, 2 * W:3 * W]
    wa = xs[:, 3 * W:3 * W + RWKV_DECAY_RANK + RWKV_ICL_RANK]
    gl = xs[:, 3 * W + RWKV_DECAY_RANK + RWKV_ICL_RANK:]
    w_log = -_softplus(-(w0_ref[...] + _mm(jnp.tanh(wa), w2_ref[...]))) - 0.5
    a = _sigmoid(a0_ref[...] + _mm(wa, a2_ref[...]))
    g = _mm(_sigmoid(gl), g2_ref[...])
    kkraw = k * kk_ref[...]
    n2 = _segsum(kkraw * kkraw, ones_ref[...])
    kk = kkraw / jnp.maximum(jnp.sqrt(n2), 1e-12)
    r_o[0] = r
    lw_o[0] = -jnp.exp(w_log)
    kk_o[0] = kk
    b_o[0] = kk * a
    v_o[0] = v
    k_o[0] = k * (1.0 + (a - 1.0) * ka_ref[...])
    g_o[0] = g


def _rwkv_prep(pa, prev, p):
    Bn, L, _ = pa.shape
    RIN = prev.shape[2]
    W = p["W"]
    tm = _row_tile(L, 512)
    row = pl.BlockSpec((1, tm, RIN), lambda b, i: (b, i, 0))
    vec = lambda n: pl.BlockSpec((1, n), lambda b, i: (0, 0))
    mat = lambda s: pl.BlockSpec(s, lambda b, i: (0, 0))
    ospec = pl.BlockSpec((1, tm, W), lambda b, i: (b, i, 0))
    oshape = jax.ShapeDtypeStruct((Bn, L, W), F32)
    return pl.pallas_call(
        functools.partial(_rwkv_prep_kernel, W=W), grid=(Bn, L // tm),
        in_specs=[row, row, vec(RIN), vec(W), vec(W), vec(W), vec(W),
                  mat(p["w2pad"].shape), mat(p["a2pad"].shape), mat(p["g2"].shape), mat((W, W))],
        out_specs=[ospec] * 7, out_shape=[oshape] * 7,
        compiler_params=_cp(("parallel", "parallel")), name="rwkv_prep")(
            pa, prev, p["mu"], p["w0"], p["a0"], p["k_k"], p["k_a"], p["w2pad"], p["a2pad"], p["g2"], p["ones_bd"])


def _rwkv_chunk_kernel(r_ref, lw_ref, kk_ref, b_ref, v_ref, k_ref, h0_ref, y_ref, h_ref, *, T, H):
    @pl.when(pl.program_id(1) == 0)
    def _():
        h_ref[...] = h0_ref[...]

    row = lax.broadcasted_iota(jnp.int32, (T, T), 0)
    col = lax.broadcasted_iota(jnp.int32, (T, T), 1)
    strict = col < row
    incl = col <= row
    blk16 = (row // 16) == (col // 16)
    blk32 = (row // 32) == (col // 32)
    eye = jnp.where(row == col, 1.0, 0.0).astype(F32)

    lw = lw_ref[0]
    G = _mm_exact_lhs(incl.astype(BF16), lw, 3)
    GT = G[T - 1:T, :]
    eG = jnp.exp(G)
    einv = jnp.exp(-G)
    eTs = jnp.exp(GT - G)
    eGT = jnp.exp(GT)
    rt = r_ref[0] * eG
    kkt = kk_ref[0] * jnp.exp(G - lw)
    bh = b_ref[0] * einv
    kh = k_ref[0] * einv
    bT = b_ref[0] * eTs
    kT = k_ref[0] * eTs
    v = v_ref[0]

    hs = range(H)
    sls = [slice(HEAD_DIM * h, HEAD_DIM * (h + 1)) for h in hs]
    per_head = lambda f: [f(h) for h in hs]
    kkt_h = per_head(lambda h: kkt[:, sls[h]])
    rt_h = per_head(lambda h: rt[:, sls[h]])
    v_h = per_head(lambda h: v[:, sls[h]])
    x2 = per_head(lambda h: jnp.concatenate([kkt_h[h], rt_h[h]], axis=0))
    m_b = per_head(lambda h: _mm_nt(x2[h], bh[:, sls[h]]))
    m_k = per_head(lambda h: _mm_nt(x2[h], kh[:, sls[h]]))
    a_ab = per_head(lambda h: jnp.where(strict, m_b[h][:T], 0.0))
    a_ak = per_head(lambda h: jnp.where(strict, m_k[h][:T], 0.0))
    a_rb = per_head(lambda h: jnp.where(incl, m_b[h][T:], 0.0))
    a_rk = per_head(lambda h: jnp.where(incl, m_k[h][T:], 0.0))
    n1 = per_head(lambda h: jnp.where(blk16, -a_ab[h], 0.0))
    inv = per_head(lambda h: eye + n1[h])
    nk = n1
    for _ in range(3):
        nk = per_head(lambda h: _mm(nk[h], nk[h]))
        inv = per_head(lambda h: inv[h] + _mm(inv[h], nk[h]))
    off32 = per_head(lambda h: jnp.where(jnp.logical_and(blk32, jnp.logical_not(blk16)), a_ab[h], 0.0))
    t32 = per_head(lambda h: _mm(inv[h], off32[h]))
    inv = per_head(lambda h: inv[h] - _mm(t32[h], inv[h]))
    off64 = per_head(lambda h: jnp.where(blk32, 0.0, a_ab[h]))
    t64 = per_head(lambda h: _mm(inv[h], off64[h]))
    inv = per_head(lambda h: inv[h] - _mm(t64[h], inv[h]))
    akv = per_head(lambda h: _mm(a_ak[h], v_h[h]))
    yv = per_head(lambda h: _mm(a_rk[h], v_h[h]))
    bTt = per_head(lambda h: bT[:, sls[h]].T)
    hv = per_head(lambda h: _mm(kT[:, sls[h]].T, v_h[h]))
    decay_col = per_head(lambda h: jnp.broadcast_to(eGT[:, sls[h]], (HEAD_DIM, HEAD_DIM)).T)

    hh = per_head(lambda h: h_ref[0, h])
    x2h = per_head(lambda h: _mm(x2[h], hh[h]))
    u = per_head(lambda h: -_mm(inv[h], x2h[h][:T] + akv[h]))
    ys = per_head(lambda h: x2h[h][T:] + _mm(a_rb[h], u[h]) + yv[h])
    hn = per_head(lambda h: decay_col[h] * hh[h] + _mm(bTt[h], u[h]) + hv[h])
    for h in hs:
        h_ref[0, h] = hn[h]
    y_ref[0] = jnp.concatenate(ys, axis=1)


def _rwkv_chunk(r, lw, kk, b, v, k, h0):
    Bn, L, W = r.shape
    H = W // HEAD_DIM
    T = RWKV_CHUNK
    assert L % T == 0
    row = pl.BlockSpec((1, T, W), lambda bi, c: (bi, c, 0))
    st = pl.BlockSpec((1, H, HEAD_DIM, HEAD_DIM), lambda bi, c: (bi, 0, 0, 0))
    return pl.pallas_call(
        functools.partial(_rwkv_chunk_kernel, T=T, H=H), grid=(Bn, L // T),
        in_specs=[row] * 6 + [st],
        out_specs=[row, st],
        out_shape=[jax.ShapeDtypeStruct((Bn, L, W), F32),
                   jax.ShapeDtypeStruct((Bn, H, HEAD_DIM, HEAD_DIM), F32)],
        compiler_params=_cp(("parallel", "arbitrary")), name="rwkv_chunk")(r, lw, kk, b, v, k, h0)


def _rwkv_step_kernel(s_ref, r_ref, lw_ref, kk_ref, b_ref, v_ref, k_ref, y_ref, so_ref):
    S = s_ref[0]
    r, kk, b, v, k = r_ref[0], kk_ref[0], b_ref[0], v_ref[0], k_ref[0]
    w = jnp.exp(lw_ref[0])
    i0 = lax.broadcasted_iota(jnp.int32, (HEAD_DIM, HEAD_DIM), 0)
    i1 = lax.broadcasted_iota(jnp.int32, (HEAD_DIM, HEAD_DIM), 1)
    eye = jnp.where(i0 == i1, 1.0, 0.0).astype(F32)
    sa = jnp.sum(S * kk, axis=-1, keepdims=True)
    vcol = jnp.sum(eye * v, axis=-1, keepdims=True)
    Sn = S * w - sa * b + vcol * k
    ycol = jnp.sum(Sn * r, axis=-1, keepdims=True)
    y_ref[0] = jnp.sum(eye * ycol, axis=-2, keepdims=True)
    so_ref[0] = Sn


def _rwkv_step(S0, r, lw, kk, b, v, k):
    Bn, H = S0.shape[0], S0.shape[1]
    hv = lambda t: t.reshape(Bn, H, 1, HEAD_DIM)
    vs = pl.BlockSpec((1, H, 1, HEAD_DIM), lambda bi: (bi, 0, 0, 0))
    ss = pl.BlockSpec((1, H, HEAD_DIM, HEAD_DIM), lambda bi: (bi, 0, 0, 0))
    y, Sn = pl.pallas_call(
        _rwkv_step_kernel, grid=(Bn,),
        in_specs=[ss] + [vs] * 6, out_specs=[vs, ss],
        out_shape=[jax.ShapeDtypeStruct((Bn, H, 1, HEAD_DIM), F32),
                   jax.ShapeDtypeStruct((Bn, H, HEAD_DIM, HEAD_DIM), F32)],
        compiler_params=_cp(("parallel",)), name="rwkv_step")(S0, hv(r), hv(lw), hv(kk), hv(b), hv(v), hv(k))
    return y.reshape(Bn, H * HEAD_DIM), Sn


def _rwkv_post_kernel(y_ref, r_ref, k_ref, v_ref, g_ref, lng_ref, lnb_ref, rk_ref, ones_ref, o_ref):
    ones = ones_ref[...]
    y = y_ref[0]
    inv_n = 1.0 / HEAD_DIM
    mu = _segsum(y, ones) * inv_n
    d = y - mu
    var = _segsum(d * d, ones) * inv_n
    yn = d * lax.rsqrt(var + RWKV_GN_EPS) * lng_ref[...] + lnb_ref[...]
    bonus = _segsum(r_ref[0] * k_ref[0] * rk_ref[...], ones) * v_ref[0]
    o_ref[0] = (yn + bonus) * g_ref[0]


def _rwkv_post(y, r, k, v, g, p):
    Bn, L, W = y.shape
    tm = _row_tile(L, 512)
    row = pl.BlockSpec((1, tm, W), lambda b, i: (b, i, 0))
    vec = pl.BlockSpec((1, W), lambda b, i: (0, 0))
    return pl.pallas_call(
        _rwkv_post_kernel, grid=(Bn, L // tm),
        in_specs=[row] * 5 + [vec] * 3 + [pl.BlockSpec((W, W), lambda b, i: (0, 0))],
        out_specs=row, out_shape=jax.ShapeDtypeStruct((Bn, L, W), F32),
        compiler_params=_cp(("parallel", "parallel")), name="rwkv_post")(
            y, r, k, v, g, p["ln_g"], p["ln_b"], p["r_k"], p["ones_bd"])


def _s5_in(u, wb_re_ref, wb_im_ref, nblk):
    ub = u.astype(BF16)
    res_re, res_im = [], []
    for a in range(nblk):
        ua = ub[:, LANES * a:LANES * (a + 1)]
        res_re.append(jnp.dot(ua, wb_re_ref[a], preferred_element_type=F32))
        res_im.append(jnp.dot(ua, wb_im_ref[a], preferred_element_type=F32))
    return jnp.concatenate(res_re, axis=1), jnp.concatenate(res_im, axis=1)


def _s5_out(h_re, h_im, u, wc_re_ref, wc_im_ref, d_ref, wglu_ref, bglu_ref, nblk):
    hr = h_re.astype(BF16)
    hi = h_im.astype(BF16)
    spb = hr.shape[1] // nblk
    ys = []
    for a in range(nblk):
        sl = slice(spb * a, spb * (a + 1))
        ys.append(jnp.dot(hr[:, sl], wc_re_ref[a], preferred_element_type=F32)
                  - jnp.dot(hi[:, sl], wc_im_ref[a], preferred_element_type=F32))
    y = jnp.concatenate(ys, axis=1) + d_ref[...] * u
    yg = _gelu_tanh(y)
    return yg * _sigmoid(_mm(yg, wglu_ref[...]) + bglu_ref[...])


def _s5_scan_kernel(u_ref, h0re_ref, h0im_ref, wbre_ref, wbim_ref, wcre_ref, wcim_ref, d_ref, wglu_ref, bglu_ref,
                    p2re_ref, p2im_ref, p8re_ref, p8im_ref, o_ref, hre_o, him_o, *, T, nblk):
    @pl.when(pl.program_id(1) == 0)
    def _():
        hre_o[0] = h0re_ref[0]
        him_o[0] = h0im_ref[0]

    u = u_ref[0]
    bre, bim = _s5_in(u, wbre_ref, wbim_ref, nblk)
    NS = bre.shape[1]
    ng = T // SUBLANES
    xr = bre.reshape(ng, SUBLANES, NS)
    xi = bim.reshape(ng, SUBLANES, NS)
    sub = lax.broadcasted_iota(jnp.int32, (1, SUBLANES, 1), 1)
    for s, d in enumerate((1, 2, 4)):
        ar = p2re_ref[s:s + 1, :].reshape(1, 1, NS)
        ai = p2im_ref[s:s + 1, :].reshape(1, 1, NS)
        keep = sub >= d
        sr = jnp.where(keep, pltpu.roll(xr, d, 1), 0.0)
        si = jnp.where(keep, pltpu.roll(xi, d, 1), 0.0)
        xr, xi = xr + ar * sr - ai * si, xi + ar * si + ai * sr
    p8r = p8re_ref[...]
    p8i = p8im_ref[...]
    cr = hre_o[0]
    ci = him_o[0]
    outs_r, outs_i = [], []
    for gidx in range(ng):
        hr = xr[gidx] + p8r * cr - p8i * ci
        hi = xi[gidx] + p8r * ci + p8i * cr
        outs_r.append(hr)
        outs_i.append(hi)
        cr = hr[SUBLANES - 1:SUBLANES, :]
        ci = hi[SUBLANES - 1:SUBLANES, :]
    hre_o[0] = cr
    him_o[0] = ci
    h_re = jnp.concatenate(outs_r, axis=0)
    h_im = jnp.concatenate(outs_i, axis=0)
    o_ref[0] = _s5_out(h_re, h_im, u, wcre_ref, wcim_ref, d_ref, wglu_ref, bglu_ref, nblk)


def _s5_prompt(proj, col, W, p):
    Bn, L, _ = proj.shape
    NS = p["ab_re"].shape[1]
    T = _row_tile(L, 128)
    nblk = W // LANES
    assert col % W == 0
    u = proj
    u_spec = pl.BlockSpec((1, T, W), lambda b, c: (b, c, col // W))
    row = pl.BlockSpec((1, T, W), lambda b, c: (b, c, 0))
    st = pl.BlockSpec((1, 1, NS), lambda b, c: (b, 0, 0))
    cst = lambda a: pl.BlockSpec(a.shape, lambda b, c: (0,) * a.ndim)
    h0 = jnp.zeros((Bn, 1, NS), F32)
    consts = [p[k] for k in ("wb_re", "wb_im", "wc_re", "wc_im", "d", "w_glu", "b_glu",
                             "pow2_re", "pow2_im", "pow8_re", "pow8_im")]
    return pl.pallas_call(
        functools.partial(_s5_scan_kernel, T=T, nblk=nblk), grid=(Bn, L // T),
        in_specs=[u_spec, st, st] + [cst(a) for a in consts],
        out_specs=[row, st, st],
        out_shape=[jax.ShapeDtypeStruct((Bn, L, W), F32),
                   jax.ShapeDtypeStruct((Bn, 1, NS), F32), jax.ShapeDtypeStruct((Bn, 1, NS), F32)],
        compiler_params=_cp(("parallel", "arbitrary")), name="s5_scan")(u, h0, h0, *consts)


def _s5_step_kernel(u_ref, h0re_ref, h0im_ref, abre_ref, abim_ref, wbre_ref, wbim_ref, wcre_ref, wcim_ref,
                    d_ref, wglu_ref, bglu_ref, o_ref, hre_o, him_o, *, nblk):
    u = u_ref[...]
    bre, bim = _s5_in(u, wbre_ref, wbim_ref, nblk)
    ar, ai = abre_ref[...], abim_ref[...]
    r0, i0 = h0re_ref[...], h0im_ref[...]
    h_re = bre + ar * r0 - ai * i0
    h_im = bim + ar * i0 + ai * r0
    hre_o[...] = h_re
    him_o[...] = h_im
    o_ref[...] = _s5_out(h_re, h_im, u, wcre_ref, wcim_ref, d_ref, wglu_ref, bglu_ref, nblk)


def _s5_step(u, h0_re, h0_im, p):
    rows, W = u.shape
    NS = p["ab_re"].shape[1]
    nblk = W // LANES
    consts = [p[k] for k in ("ab_re", "ab_im", "wb_re", "wb_im", "wc_re", "wc_im", "d", "w_glu", "b_glu")]
    args = [u, h0_re, h0_im] + consts
    full = lambda a: pl.BlockSpec(a.shape, lambda i: (0,) * a.ndim)
    return pl.pallas_call(
        functools.partial(_s5_step_kernel, nblk=nblk), grid=(1,),
        in_specs=[full(a) for a in args],
        out_specs=[full(u), full(h0_re), full(h0_im)],
        out_shape=[jax.ShapeDtypeStruct((rows, W), F32), jax.ShapeDtypeStruct((rows, NS), F32),
                   jax.ShapeDtypeStruct((rows, NS), F32)],
        compiler_params=_cp(("arbitrary",)), name="s5_step")(*args)


def _attn_prep_kernel(*refs, normrope, emit_kv, W):
    refs = list(refs)
    q_ref, k_ref, v_ref = refs[:3]
    if normrope:
        qn_ref, kn_ref, cos_ref, sin_ref, ones_ref = refs[3:8]
    outs = refs[8:] if normrope else refs[3:]
    q_o = outs.pop(0)
    if normrope:
        k_o = outs.pop(0)
    if emit_kv:
        kb_o, vt_o = outs[0], outs[1]
        if normrope:
            km_o = outs[2]
    q, k = q_ref[0], k_ref[0]
    if normrope:
        ones = ones_ref[...]
        cosf, sinf = cos_ref[...], sin_ref[...]
        dmod = lax.broadcasted_iota(jnp.int32, (1, W), 1) % HEAD_DIM
        half = ROT_DIM // 2

        def nr(x, g):
            ms = _segsum(x * x, ones) * (1.0 / HEAD_DIM)
            xn = x * lax.rsqrt(ms + RMS_EPS) * g
            up = pltpu.roll(xn, W - half, 1)
            dn = pltpu.roll(xn, half, 1)
            rot = jnp.where(dmod < half, -up, jnp.where(dmod < ROT_DIM, dn, 0.0))
            return xn * cosf + rot * sinf

        q = nr(q, qn_ref[...])
        k = nr(k, kn_ref[...])
        k_o[0] = k
    q_o[0] = q
    if emit_kv:
        kb_o[0] = k.astype(BF16)
        vt_o[0] = v_ref[0].T.astype(BF16)
        if normrope:
            km_o[0, 0] = jnp.mean(k, axis=0, keepdims=True)


def _attn_prep(pq, col0, W, normrope, emit_kv, p=None, cosf=None, sinf=None):
    Bn, L, _ = pq.shape
    assert col0 % W == 0
    tm = _row_tile(L, MOBA_BLOCK)
    nbk = L // tm
    col = lambda c: pl.BlockSpec((1, tm, W), lambda b, i, c=c + col0 // W: (b, i, c))
    row_o = pl.BlockSpec((1, tm, W), lambda b, i: (b, i, 0))
    vt_o = pl.BlockSpec((1, W, tm), lambda b, i: (b, 0, i))
    in_specs = [col(0), col(1), col(2)]
    args = [pq, pq, pq]
    out_specs = [row_o]
    out_shape = [jax.ShapeDtypeStruct((Bn, L, W), F32)]
    if normrope:
        vec = pl.BlockSpec((1, W), lambda b, i: (0, 0))
        tab = pl.BlockSpec((tm, W), lambda b, i: (i, 0))
        in_specs += [vec, vec, tab, tab, pl.BlockSpec((W, W), lambda b, i: (0, 0))]
        args += [p["q_norm"], p["k_norm"], cosf, sinf, p["ones_bd"]]
        out_specs.append(row_o)
        out_shape.append(jax.ShapeDtypeStruct((Bn, L, W), F32))
    if emit_kv:
        out_specs += [row_o, vt_o]
        out_shape += [jax.ShapeDtypeStruct((Bn, L, W), BF16), jax.ShapeDtypeStruct((Bn, W, L), BF16)]
        if normrope:
            out_specs.append(pl.BlockSpec((1, 1, 1, W), lambda b, i: (b, i, 0, 0)))
            out_shape.append(jax.ShapeDtypeStruct((Bn, nbk, 1, W), F32))
    return pl.pallas_call(
        functools.partial(_attn_prep_kernel, normrope=normrope, emit_kv=emit_kv, W=W), grid=(Bn, nbk),
        in_specs=in_specs, out_specs=out_specs, out_shape=out_shape,
        compiler_params=_cp(("parallel", "parallel")),
        name="moba_prep" if normrope else "sb_prep")(*args)


def _moba_kernel(q_ref, kb_ref, vt_ref, km_ref, o_ref, sel_scr, *, BQ, NB):
    i = pl.program_id(2)
    qT = q_ref[0].T
    hrow = lax.broadcasted_iota(jnp.int32, (LANES, 1), 0) // HEAD_DIM
    km = km_ref[0]
    nidx = lax.broadcasted_iota(jnp.int32, (NB, 1), 0)
    kpos = lax.broadcasted_iota(jnp.int32, (BQ, BQ), 0)
    qpos = lax.broadcasted_iota(jnp.int32, (BQ, BQ), 1)
    scale = HEAD_DIM ** -0.5
    qfull = []
    for hh in range(2):
        qm = jnp.where(hrow == hh, qT, 0.0)
        gate = _mm3(km, qm)
        valid = nidx < i
        g1 = jnp.where(valid, gate, NEG_INF)
        m1 = jnp.max(g1, axis=0, keepdims=True)
        g2 = jnp.where(g1 >= m1, NEG_INF, g1)
        m2 = jnp.max(g2, axis=0, keepdims=True)
        g3 = jnp.where(g2 >= m2, NEG_INF, g2)
        m3 = jnp.max(g3, axis=0, keepdims=True)
        sel_scr[hh] = jnp.where(jnp.logical_and(valid, gate >= m3), 1.0, 0.0)
        qfull.append((qm * scale).astype(BF16))

    nq = BQ // LANES
    chains = [(hh, c) for hh in range(2) for c in range(nq)]
    lsl = [slice(LANES * c, LANES * (c + 1)) for _, c in chains]
    qms = [qfull[hh][:, lsl[n]] for n, (hh, _) in enumerate(chains)]
    each = lambda f: [f(n) for n in range(len(chains))]

    kj = kb_ref[0, pl.ds(i * BQ, BQ), :]
    vj = vt_ref[0, :, pl.ds(i * BQ, BQ)]
    causal = kpos <= qpos
    s = each(lambda n: jnp.where(causal[:, lsl[n]], jnp.dot(kj, qms[n], preferred_element_type=F32), NEG_INF))
    m0 = each(lambda n: jnp.max(s[n], axis=0, keepdims=True))
    p0 = each(lambda n: jnp.exp(s[n] - m0[n]))
    l0 = each(lambda n: jnp.sum(p0[n], axis=0, keepdims=True))
    a0 = each(lambda n: jnp.dot(vj, p0[n].astype(BF16), preferred_element_type=F32))

    def past(j, nblk, st):
        m_old, l_old, a_old = st
        kj = kb_ref[0, pl.ds(j * BQ, nblk * BQ), :]
        vj = vt_ref[0, :, pl.ds(j * BQ, nblk * BQ)]
        selrow = [[sel_scr[hh, pl.ds(j + t, 1), :] > 0.5 for t in range(nblk)] for hh in range(2)]
        sel = each(lambda n: jnp.concatenate(
            [jnp.broadcast_to(selrow[chains[n][0]][t][:, lsl[n]], (BQ, LANES)) for t in range(nblk)], axis=0))
        s = each(lambda n: jnp.where(sel[n], jnp.dot(kj, qms[n], preferred_element_type=F32), NEG_INF))
        m_new = each(lambda n: jnp.maximum(m_old[n], jnp.max(s[n], axis=0, keepdims=True)))
        alpha = each(lambda n: jnp.exp(m_old[n] - m_new[n]))
        pexp = each(lambda n: jnp.exp(s[n] - m_new[n]))
        l_new = each(lambda n: alpha[n] * l_old[n] + jnp.sum(pexp[n], axis=0, keepdims=True))
        a_new = each(lambda n: alpha[n] * a_old[n]
                     + jnp.dot(vj, pexp[n].astype(BF16), preferred_element_type=F32))
        return tuple(m_new), tuple(l_new), tuple(a_new)

    odd = i % 2
    st = lax.fori_loop(0, odd, lambda t, st: past(0, 1, st), (tuple(m0), tuple(l0), tuple(a0)))
    _, l_fin, a_fin = lax.fori_loop(0, i // 2, lambda t, st: past(odd + 2 * t, 2, st), st)
    outs = [jnp.concatenate([a_fin[hh * nq + c] / l_fin[hh * nq + c] for c in range(nq)], axis=1)
            for hh in range(2)]
    o_ref[0] = jnp.where(hrow == 0, outs[0], outs[1]).T


def _moba_prompt(q, kb, vt, kmean):
    Bn, L, W = q.shape
    BQ = MOBA_BLOCK
    assert L % BQ == 0
    NB = L // BQ
    NBP = -(-NB // SUBLANES) * SUBLANES
    km = kmean.reshape(Bn, NB, W)
    if NBP != NB:
        km = jnp.pad(km, ((0, 0), (0, NBP - NB), (0, 0)))
    npair = W // LANES
    return pl.pallas_call(
        functools.partial(_moba_kernel, BQ=BQ, NB=NBP), grid=(Bn, npair, NB),
        in_specs=[pl.BlockSpec((1, BQ, LANES), lambda b, p, i: (b, i, p)),
                  pl.BlockSpec((1, L, LANES), lambda b, p, i: (b, 0, p)),
                  pl.BlockSpec((1, LANES, L), lambda b, p, i: (b, p, 0)),
                  pl.BlockSpec((1, NBP, LANES), lambda b, p, i: (b, 0, p))],
        out_specs=pl.BlockSpec((1, BQ, LANES), lambda b, p, i: (b, i, p)),
        out_shape=jax.ShapeDtypeStruct((Bn, L, W), F32),
        scratch_shapes=[pltpu.VMEM((2, NBP, BQ), F32)],
        compiler_params=_cp(("parallel", "parallel", "arbitrary")), name="moba_attn")(q, kb, vt, km)


def _sb_terms(z):
    sp = jnp.log(1.0 + jnp.exp(-jnp.abs(z)))
    return jnp.minimum(z, 0.0) - sp, jnp.minimum(-z, 0.0) - sp


def _sb_kernel(q_ref, kb_ref, vt_ref, o_ref, *, BQ):
    i = pl.program_id(2)
    qT = q_ref[0].T
    hrow = lax.broadcasted_iota(jnp.int32, (LANES, 1), 0) // HEAD_DIM
    kpos = lax.broadcasted_iota(jnp.int32, (BQ, BQ), 0)
    qpos = lax.broadcasted_iota(jnp.int32, (BQ, BQ), 1)
    later = (qpos > kpos).astype(BF16)
    scale = HEAD_DIM ** -0.5
    nq = BQ // LANES
    chains = [(hh, c) for hh in range(2) for c in range(nq)]
    qfull = [(jnp.where(hrow == hh, qT, 0.0) * scale).astype(BF16) for hh in range(2)]
    qms = [qfull[hh][:, LANES * c:LANES * (c + 1)] for hh, c in chains]
    oks = [(kpos < qpos)[:, LANES * c:LANES * (c + 1)] for _, c in chains]
    each = lambda f: [f(n) for n in range(len(chains))]

    def scores(j, diag):
        kj = kb_ref[0, pl.ds(j * BQ, BQ), :]
        vj = vt_ref[0, :, pl.ds(j * BQ, BQ)]
        z = each(lambda n: jnp.dot(kj, qms[n], preferred_element_type=F32))
        terms = each(lambda n: _sb_terms(z[n]))
        ls = each(lambda n: terms[n][0])
        lneg = each(lambda n: jnp.where(oks[n], terms[n][1], 0.0) if diag else terms[n][1])
        within = each(lambda n: _mm_exact_lhs(later, lneg[n], 2))
        tot = each(lambda n: jnp.sum(lneg[n], axis=0, keepdims=True))
        return ls, within, tot, vj

    def accumulate(sc, diag, carry, acc):
        ls, within, tot, vj = sc
        wgt = each(lambda n: jnp.exp(ls[n] + within[n] + carry[n]))
        if diag:
            wgt = each(lambda n: jnp.where(oks[n], wgt[n], 0.0))
        carry = each(lambda n: carry[n] + tot[n])
        acc = each(lambda n: acc[n] + jnp.dot(vj, wgt[n].astype(BF16), preferred_element_type=F32))
        return carry, acc

    carry0 = each(lambda n: jnp.zeros((1, LANES), F32))
    acc0 = each(lambda n: jnp.zeros((LANES, LANES), F32))
    state = accumulate(scores(i, True), True, carry0, acc0)
    state = (tuple(state[0]), tuple(state[1]))

    def one(t, st):
        carry, acc = accumulate(scores(i - 1, False), False, list(st[0]), list(st[1]))
        return tuple(carry), tuple(acc)

    def two(t, st):
        j = i - 1 - odd - 2 * t
        s1, s2 = scores(j, False), scores(j - 1, False)
        carry, acc = accumulate(s1, False, list(st[0]), list(st[1]))
        carry, acc = accumulate(s2, False, carry, acc)
        return tuple(carry), tuple(acc)

    odd = i % 2
    state = lax.fori_loop(0, odd, one, state)
    _, acc = lax.fori_loop(0, i // 2, two, state)
    outs = [jnp.concatenate([acc[hh * nq + c] for c in range(nq)], axis=1) for hh in range(2)]
    o_ref[0] = jnp.where(hrow == 0, outs[0], outs[1]).T


def _sb_prompt(q, kb, vt):
    Bn, L, W = q.shape
    BQ = _row_tile(L, MOBA_BLOCK)
    npair = W // LANES
    return pl.pallas_call(
        functools.partial(_sb_kernel, BQ=BQ), grid=(Bn, npair, L // BQ),
        in_specs=[pl.BlockSpec((1, BQ, LANES), lambda b, p, i: (b, i, p)),
                  pl.BlockSpec((1, L, LANES), lambda b, p, i: (b, 0, p)),
                  pl.BlockSpec((1, LANES, L), lambda b, p, i: (b, p, 0))],
        out_specs=pl.BlockSpec((1, BQ, LANES), lambda b, p, i: (b, i, p)),
        out_shape=jax.ShapeDtypeStruct((Bn, L, W), F32),
        compiler_params=_cp(("parallel", "parallel", "arbitrary")), name="sb_attn")(q, kb, vt)


PAGES_PER_STEP = 8


def _pool_view(cache):
    return jnp.transpose(cache, (0, 1, 3, 4, 2))


def _col_bcast(x, H, page):
    DB = x.shape[0]
    return jnp.broadcast_to(x.reshape(DB, H, HEAD_DIM, 1), (DB, H, HEAD_DIM, page))


def _page_specs(l, n, H, page, nsteps, order_desc):
    specs = []
    for t in range(n):
        if order_desc:
            imap = lambda b, s, pt, t=t: (l, pt[b, (nsteps - 1 - s) * n + t], 0, 0, 0)
        else:
            imap = lambda b, s, pt, t=t: (l, pt[b, s * n + t], 0, 0, 0)
        specs.append(pl.BlockSpec((1, 1, H, HEAD_DIM, page), imap))
    return specs


def _moba_gate_kernel(pt_ref, *refs, n, n_pages, ppb):
    k_refs = refs[:n]
    qc_ref, idx_ref, g_scr = refs[n:]
    s = pl.program_id(1)
    qc = qc_ref[0]
    lane = lax.broadcasted_iota(jnp.int32, g_scr.shape, 1)

    @pl.when(s == 0)
    def _():
        g_scr[...] = jnp.zeros_like(g_scr)

    nblk = n // ppb
    ksum = [sum((k_refs[blk * ppb + t][0, 0] for t in range(1, ppb)), k_refs[blk * ppb][0, 0])
            for blk in range(nblk)]
    z = [jnp.sum(ksum[blk] * qc, axis=1) for blk in range(nblk)]
    zs = [jnp.sum(z[blk], axis=-1, keepdims=True) for blk in range(nblk)]
    g = g_scr[...]
    for blk in range(nblk):
        g = jnp.where(lane == s * nblk + blk, zs[blk], g)
    g_scr[...] = g

    @pl.when(s == pl.num_programs(1) - 1)
    def _():
        NBK = n_pages // ppb
        gate = g * (1.0 / MOBA_BLOCK)
        nidx = lax.broadcasted_iota(jnp.int32, gate.shape, 1)
        olane = lax.broadcasted_iota(jnp.int32, idx_ref.shape[1:], 1)
        out = jnp.zeros(idx_ref.shape[1:], jnp.int32)
        for r in range(MOBA_TOPK):
            m = jnp.max(gate, axis=-1, keepdims=True)
            am = jnp.min(jnp.where(gate >= m, nidx, NBK), axis=-1, keepdims=True)
            out = jnp.where(olane == r, am, out)
            gate = jnp.where(nidx == am, NEG_INF, gate)
        idx_ref[0] = out


def _moba_sample_select(l, pool_k, page_table, q):
    _, NP, H, Dh, page = pool_k.shape
    DB, n_pages = page_table.shape
    n = PAGES_PER_STEP
    ppb = MOBA_BLOCK // page
    assert n_pages % n == 0 and n % ppb == 0
    gs = pltpu.PrefetchScalarGridSpec(
        num_scalar_prefetch=1, grid=(DB, n_pages // n),
        in_specs=_page_specs(l, n, H, page, n_pages // n, False)
        + [pl.BlockSpec((1, H, Dh, page), lambda b, s, pt: (b, 0, 0, 0))],
        out_specs=pl.BlockSpec((1, H, LANES), lambda b, s, pt: (b, 0, 0)),
        scratch_shapes=[pltpu.VMEM((H, n_pages // ppb), F32)])
    return pl.pallas_call(
        functools.partial(_moba_gate_kernel, n=n, n_pages=n_pages, ppb=ppb), grid_spec=gs,
        out_shape=jax.ShapeDtypeStruct((DB, H, LANES), jnp.int32),
        compiler_params=_cp(("parallel", "arbitrary")), name="moba_sample_gate")(
            page_table, *([pool_k] * n), _col_bcast(q, H, page))


def _moba_sample_attn_kernel(sel_ref, pt_ref, *refs, n_pg):
    k_refs, v_refs = refs[:n_pg], refs[n_pg:2 * n_pg]
    qc_ref, kn_ref, vn_ref, o_ref = refs[2 * n_pg:]
    qc = qc_ref[0, 0] * (HEAD_DIM ** -0.5)
    s_new = jnp.sum(qc[:, 0:1] * kn_ref[0, 0], axis=0, keepdims=True)
    ss = [jnp.sum(kr[0, 0, 0] * qc, axis=0, keepdims=True) for kr in k_refs]
    m = s_new
    for s in ss:
        m = jnp.maximum(m, jnp.max(s, axis=-1, keepdims=True))
    p_new = jnp.exp(s_new - m)
    den = p_new
    acc = jnp.zeros(qc.shape, F32)
    for s, vr in zip(ss, v_refs):
        pe = jnp.exp(s - m)
        den = den + jnp.sum(pe, axis=-1, keepdims=True)
        acc = acc + vr[0, 0, 0] * pe
    o_ref[0, 0] = (jnp.sum(acc, axis=-1, keepdims=True) + p_new * vn_ref[0, 0]) / den


def _moba_sample_attn(l, sel, page_table, pool_k, pool_v, q, k_new, v_new):
    DB, H, topk = sel.shape
    _, NP, _, Dh, page = pool_k.shape
    ppb = MOBA_BLOCK // page
    n_pg = topk * ppb

    def pspec(t):
        r, half = divmod(t, ppb)
        return pl.BlockSpec(
            (1, 1, 1, Dh, page),
            lambda b, h, sel_r, pt, r=r, half=half: (l, pt[b, sel_r[b, h, r] * ppb + half], h, 0, 0))

    col = pl.BlockSpec((1, 1, Dh, 1), lambda b, h, sel_r, pt: (b, h, 0, 0))
    in_specs = ([pspec(t) for t in range(n_pg)] * 2
                + [pl.BlockSpec((1, 1, Dh, page), lambda b, h, sel_r, pt: (b, h, 0, 0)), col, col])
    c4 = lambda t: t.reshape(DB, H, Dh, 1)
    gs = pltpu.PrefetchScalarGridSpec(num_scalar_prefetch=2, grid=(DB, H), in_specs=in_specs, out_specs=col)
    out = pl.pallas_call(
        functools.partial(_moba_sample_attn_kernel, n_pg=n_pg), grid_spec=gs,
        out_shape=jax.ShapeDtypeStruct((DB, H, Dh, 1), F32),
        compiler_params=_cp(("parallel", "arbitrary")), name="moba_sample_attn")(
            sel, page_table, *([pool_k] * n_pg), *([pool_v] * n_pg), _col_bcast(q, H, page), c4(k_new), c4(v_new))
    return out.reshape(DB, H * Dh)


def _sb_sample_kernel(pt_ref, *refs, n, page):
    k_refs, v_refs = refs[:n], refs[n:2 * n]
    qc_ref, o_ref, carry_scr, acc_scr = refs[2 * n:]
    s = pl.program_id(1)

    @pl.when(s == 0)
    def _():
        carry_scr[...] = jnp.zeros_like(carry_scr)
        acc_scr[...] = jnp.zeros_like(acc_scr)

    r0 = lax.broadcasted_iota(jnp.int32, (page, page), 0)
    r1 = lax.broadcasted_iota(jnp.int32, (page, page), 1)
    later = (r0 > r1).astype(BF16)
    H = qc_ref.shape[1]
    scale = HEAD_DIM ** -0.5
    zrows = [[None] * H for _ in range(n)]
    for h in range(H):
        qh = qc_ref[0, h] * scale
        for t in range(n):
            zrows[t][h] = jnp.sum(k_refs[t][0, 0, h] * qh, axis=0, keepdims=True)
    z = [jnp.concatenate(zrows[t], axis=0) for t in range(n)]
    terms = [_sb_terms(z[t]) for t in range(n)]
    within = [_mm_exact_rhs(terms[t][1], later, 2) for t in range(n)]
    tot = [jnp.sum(terms[t][1], axis=-1, keepdims=True) for t in range(n)]
    carry = carry_scr[...]
    wgt = [None] * n
    for t in reversed(range(n)):
        wgt[t] = jnp.exp(terms[t][0] + within[t] + carry)
        carry = carry + tot[t]
    carry_scr[...] = carry
    for h in range(H):
        a = acc_scr[h]
        for t in range(n):
            a = a + v_refs[t][0, 0, h] * wgt[t][h:h + 1, :]
        acc_scr[h] = a

    @pl.when(s == pl.num_programs(1) - 1)
    def _():
        o_ref[0] = jnp.sum(acc_scr[...], axis=-1, keepdims=True)


def _sb_sample(l, pool_k, pool_v, page_table, q):
    _, NP, H, Dh, page = pool_k.shape
    DB, n_pages = page_table.shape
    n = PAGES_PER_STEP
    assert n_pages % n == 0
    nsteps = n_pages // n
    in_specs = (_page_specs(l, n, H, page, nsteps, True) + _page_specs(l, n, H, page, nsteps, True)
                + [pl.BlockSpec((1, H, Dh, page), lambda b, s, pt: (b, 0, 0, 0))])
    gs = pltpu.PrefetchScalarGridSpec(
        num_scalar_prefetch=1, grid=(DB, nsteps), in_specs=in_specs,
        out_specs=pl.BlockSpec((1, H, Dh, 1), lambda b, s, pt: (b, 0, 0, 0)),
        scratch_shapes=[pltpu.VMEM((H, 1), F32), pltpu.VMEM((H, Dh, page), F32)])
    out = pl.pallas_call(
        functools.partial(_sb_sample_kernel, n=n, page=page), grid_spec=gs,
        out_shape=jax.ShapeDtypeStruct((DB, H, Dh, 1), F32),
        compiler_params=_cp(("parallel", "arbitrary")), name="sb_sample")(
            page_table, *([pool_k] * n), *([pool_v] * n), _col_bcast(q, H, page))
    return out.reshape(DB, H * Dh)


def _rope_tables(pos, W):
    half = ROT_DIM // 2
    inv = ROPE_THETA ** (-jnp.arange(half, dtype=F32) * 2.0 / ROT_DIM)
    ang = pos.astype(F32)[:, None] * inv[None, :]
    n = pos.shape[0]
    pad_c = jnp.ones((n, HEAD_DIM - ROT_DIM), F32)
    pad_s = jnp.zeros((n, HEAD_DIM - ROT_DIM), F32)
    cos_h = jnp.concatenate([jnp.cos(ang), jnp.cos(ang), pad_c], axis=1)
    sin_h = jnp.concatenate([jnp.sin(ang), jnp.sin(ang), pad_s], axis=1)
    reps = W // HEAD_DIM
    return jnp.tile(cos_h, (1, reps)), jnp.tile(sin_h, (1, reps))


def _block_diag(blocks):
    G, a, b = blocks.shape
    eye = jnp.eye(G, dtype=blocks.dtype)
    return (blocks[:, :, None, :] * eye[:, None, :, None]).reshape(G * a, G * b)


def _s5_params(l, a_re, a_im, b_re, b_im, c_re, c_im, d, log_dt, w_glu, b_glu, W):
    G, P = a_re.shape[1], a_re.shape[2]
    C = W // G
    are, aim = a_re[l], a_im[l]
    dt = jnp.exp(log_dt[l])[:, None]
    mag = jnp.exp(are * dt)
    ab_re, ab_im = mag * jnp.cos(aim * dt), mag * jnp.sin(aim * dt)
    den = are * are + aim * aim
    n_re = ab_re - 1.0
    cf_re = (n_re * are + ab_im * aim) / den
    cf_im = (ab_im * are - n_re * aim) / den
    bb_re = cf_re[..., None] * b_re[l] - cf_im[..., None] * b_im[l]
    bb_im = cf_re[..., None] * b_im[l] + cf_im[..., None] * b_re[l]
    gpb = LANES // C
    nblk = G // gpb

    def in_map(bb):
        t = jnp.swapaxes(bb, 1, 2).reshape(nblk, gpb, C, P)
        return jnp.stack([_block_diag(t[a]) for a in range(nblk)]).astype(BF16)

    def out_map(cc):
        t = jnp.swapaxes(cc, 1, 2).reshape(nblk, gpb, P, C)
        return jnp.stack([_block_diag(t[a]) for a in range(nblk)]).astype(BF16)

    def powers(ns):
        ns = jnp.asarray(ns, F32)[:, None, None]
        m = jnp.exp(are[None] * dt[None] * ns)
        ph = aim[None] * dt[None] * ns
        return (m * jnp.cos(ph)).reshape(-1, G * P), (m * jnp.sin(ph)).reshape(-1, G * P)

    p2r, p2i = powers([1.0, 2.0, 4.0])
    p8r, p8i = powers(np.arange(1, SUBLANES + 1, dtype=np.float32))
    pad = lambda t: jnp.pad(t, ((0, SUBLANES - t.shape[0]), (0, 0)))
    return dict(ab_re=ab_re.reshape(1, G * P), ab_im=ab_im.reshape(1, G * P),
                wb_re=in_map(bb_re), wb_im=in_map(bb_im),
                wc_re=out_map(c_re[l]), wc_im=out_map(c_im[l]),
                d=d[l].reshape(1, W), w_glu=w_glu[l].astype(BF16), b_glu=b_glu[l].reshape(1, W),
                pow2_re=pad(p2r), pow2_im=pad(p2i), pow8_re=p8r, pow8_im=p8i)


def _proj_cols(rin, D, W):
    gate = -(-rin // D) * D
    moba = gate + N_BRANCH * D
    sb = moba + 3 * W
    s5 = sb + 3 * W
    return dict(rin=rin, gate=gate, moba=moba, sb=sb, s5=s5, total=s5 + W)


def _finish(x, mods, lw, proj, o_a, o_b, o_c, o_d):
    x = _merge(o_a, o_b, o_c, o_d, proj, lw["cols"]["gate"], lw["w_branch"], lw["w_out"], x, mods[2])
    return _ffn(x, lw["norm_ffn_g"], mods[4], mods[3], mods[5], lw["ffn_w_up"], lw["ffn_w_down"])


def _layer_prompt(x, mods, lw):
    Bn, L, D = x.shape
    W = lw["W"]
    H = W // HEAD_DIM
    c = lw["cols"]
    proj = _inproj(x, lw["norm_mix_g"], mods[1], mods[0], lw["w_in"], "in_proj_prompt")
    pa = proj[:, :, :c["rin"]]
    prev = jnp.concatenate([jnp.zeros((Bn, 1, c["rin"]), F32), pa[:, :-1]], axis=1)
    r, lwd, kk, b, v, k, g = _rwkv_prep(proj, prev, lw["rwkv"])
    y, hfin = _rwkv_chunk(r, lwd, kk, b, v, k, jnp.zeros((Bn, H, HEAD_DIM, HEAD_DIM), F32))
    o_a = _rwkv_post(y, r, k, v, g, lw["rwkv"])
    s_fin = jnp.swapaxes(hfin, 2, 3)
    o_b, hre, him = _s5_prompt(proj, c["s5"], W, lw["s5"])
    cosf, sinf = lw["rope_p"]
    q, kf, kb, vt, kmean = _attn_prep(proj, c["moba"], W, True, True, lw["moba"], cosf, sinf)
    o_c = _moba_prompt(q, kb, vt, kmean)
    q2, kb2, vt2 = _attn_prep(proj, c["sb"], W, False, True)
    o_d = _sb_prompt(q2, kb2, vt2)
    x = _finish(x, mods, lw, proj, o_a, o_b, o_c, o_d)
    G = lw["G"]
    hd = lambda col: proj[:, :, col:col + W].reshape(Bn, L, H, HEAD_DIM)
    st = (kf.reshape(Bn, L, H, HEAD_DIM), hd(c["moba"] + 2 * W), hd(c["sb"] + W), hd(c["sb"] + 2 * W),
          s_fin, pa[:, -1], hre.reshape(Bn, G, -1), him.reshape(Bn, G, -1))
    return x, st


def _layer_sample(l, x, mods, lw, shift0, s0, h0_re, h0_im, pools, page_table):
    _, DB, D = x.shape
    W = lw["W"]
    H = W // HEAD_DIM
    G = lw["G"]
    c = lw["cols"]
    proj = _inproj(x, lw["norm_mix_g"], mods[1], mods[0], lw["w_in"], "in_proj_sample")
    cols = lambda col: proj[0, :, col:col + W]
    r, lwd, kk, b, v, k, g = _rwkv_prep(proj, shift0[None], lw["rwkv"])
    y, s_new = _rwkv_step(s0, r[0], lwd[0], kk[0], b[0], v[0], k[0])
    o_a = _rwkv_post(y[None], r, k, v, g, lw["rwkv"])
    o_b, hre, him = _s5_step(cols(c["s5"]), h0_re.reshape(DB, -1), h0_im.reshape(DB, -1), lw["s5"])
    pool_mk, pool_mv, pool_sk, pool_sv = pools
    cosf, sinf = lw["rope_s"]
    q, kf = _attn_prep(proj, c["moba"], W, True, False, lw["moba"], cosf, sinf)
    sel = _moba_sample_select(l, pool_mk, page_table, q[0])[:, :, :MOBA_TOPK]
    vnew = cols(c["moba"] + 2 * W)
    o_c = _moba_sample_attn(l, sel, page_table, pool_mk, pool_mv, q[0], kf[0], vnew)
    o_d = _sb_sample(l, pool_sk, pool_sv, page_table, cols(c["sb"]))
    x = _finish(x, mods, lw, proj, o_a, o_b[None], o_c[None], o_d[None])
    hd = lambda t: t.reshape(DB, 1, H, HEAD_DIM)
    st = (hd(kf[0]), hd(vnew), hd(cols(c["sb"] + W)), hd(cols(c["sb"] + 2 * W)),
          s_new, proj[0, :, :c["rin"]], hre.reshape(DB, G, -1), him.reshape(DB, G, -1))
    return x, st


def kernel(x_prompt, x_sample, c_prompt, c_sample, cache_moba_k, cache_moba_v, cache_sb_k, cache_sb_v, state_rwkv, state_rwkv_shift, state_s5_re, state_s5_im, page_table, norm_mix_g, norm_ffn_g, ada_w, ada_b, w_in, rwkv_mu, rwkv_w0, rwkv_w2, rwkv_a0, rwkv_a2, rwkv_g2, rwkv_k_k, rwkv_k_a, rwkv_r_k, rwkv_ln_g, rwkv_ln_b, s5_a_re, s5_a_im, s5_b_re, s5_b_im, s5_c_re, s5_c_im, s5_d, s5_log_dt, s5_w_glu, s5_b_glu, moba_q_norm, moba_k_norm, w_branch, w_out, ffn_w_up, ffn_w_down):
    depth = w_in.shape[0]
    BP, LP, D = x_prompt.shape
    DB = x_sample.shape[0]
    assert x_sample.shape[1] == 1
    W = rwkv_w0.shape[1]
    H = W // HEAD_DIM
    G = s5_a_re.shape[1]
    n_pool, page = cache_moba_k.shape[1], cache_moba_k.shape[2]
    n_pages = page_table.shape[1]
    past_len = n_pages * page
    assert past_len % MOBA_BLOCK == 0 and MOBA_BLOCK % page == 0 and past_len // MOBA_BLOCK >= MOBA_TOPK
    rwkv_in = rwkv_mu.shape[1]
    splits = [N_BRANCH * D, N_BRANCH * D + rwkv_in, N_BRANCH * D + rwkv_in + W, N_BRANCH * D + rwkv_in + 4 * W]
    cols = _proj_cols(rwkv_in, D, W)

    ones_bd = jnp.kron(jnp.eye(H, dtype=F32), jnp.ones((HEAD_DIM, HEAD_DIM), F32)).astype(BF16)
    pools = tuple(_pool_view(c) for c in (cache_moba_k, cache_moba_v, cache_sb_k, cache_sb_v))
    rope_p = _rope_tables(jnp.arange(LP, dtype=jnp.int32), W)
    rope_s = _rope_tables(jnp.full((DB,), past_len, jnp.int32), W)
    rows = BP + DB
    rows_pad = -(-rows // SUBLANES) * SUBLANES
    c_all = jnp.pad(jnp.concatenate([c_prompt, c_sample], axis=0), ((0, rows_pad - rows), (0, 0)))

    yp, ys = x_prompt, x_sample.reshape(1, DB, D)
    st_p_all, st_s_all = [], []
    for l in range(depth):
        wl = w_in[l].astype(BF16)
        w_cat = jnp.concatenate(
            [wl[:, splits[0]:splits[1]], jnp.zeros((D, cols["gate"] - rwkv_in), BF16), wl[:, :splits[0]],
             wl[:, splits[2]:splits[3]], wl[:, splits[3]:], wl[:, splits[1]:splits[2]]], axis=1)
        zpad = jnp.zeros((RWKV_DECAY_RANK, W), F32)
        lw = dict(
            W=W, G=G, norm_mix_g=norm_mix_g[l].reshape(1, D), norm_ffn_g=norm_ffn_g[l].reshape(1, D),
            cols=cols, w_in=w_cat,
            w_branch=w_branch[l].astype(BF16), w_out=w_out[l].astype(BF16),
            ffn_w_up=ffn_w_up[l].astype(BF16), ffn_w_down=ffn_w_down[l].astype(BF16),
            rope_p=rope_p, rope_s=rope_s,
            rwkv=dict(W=W, mu=rwkv_mu[l].reshape(1, -1), w0=rwkv_w0[l].reshape(1, W), a0=rwkv_a0[l].reshape(1, W),
                      k_k=rwkv_k_k[l].reshape(1, W), k_a=rwkv_k_a[l].reshape(1, W),
                      w2pad=jnp.concatenate([rwkv_w2[l], zpad], axis=0).astype(BF16),
                      a2pad=jnp.concatenate([zpad, rwkv_a2[l]], axis=0).astype(BF16),
                      g2=rwkv_g2[l].astype(BF16), ones_bd=ones_bd,
                      ln_g=rwkv_ln_g[l].reshape(1, W), ln_b=rwkv_ln_b[l].reshape(1, W),
                      r_k=rwkv_r_k[l].reshape(1, W)),
            s5=_s5_params(l, s5_a_re, s5_a_im, s5_b_re, s5_b_im, s5_c_re, s5_c_im, s5_d, s5_log_dt,
                          s5_w_glu, s5_b_glu, W),
            moba=dict(q_norm=jnp.tile(moba_q_norm[l], H).reshape(1, W),
                      k_norm=jnp.tile(moba_k_norm[l], H).reshape(1, W), ones_bd=ones_bd),
        )
        mod = _ada(c_all, ada_w[l].astype(BF16), ada_b[l])
        mods_p = [mod[:BP, i * D:(i + 1) * D].reshape(BP, 1, D) for i in range(6)]
        mods_s = [mod[BP:rows, i * D:(i + 1) * D].reshape(1, DB, D) for i in range(6)]
        yp, st_p = _layer_prompt(yp, mods_p, lw)
        ys, st_s = _layer_sample(l, ys, mods_s, lw, state_rwkv_shift[l], state_rwkv[l], state_s5_re[l],
                                 state_s5_im[l], pools, page_table)
        st_p_all.append(st_p)
        st_s_all.append(st_s)
    stk = lambda lst, i: jnp.stack([s[i] for s in lst], axis=0)
    return (yp, ys.reshape(DB, 1, D),
            stk(st_p_all, 0), stk(st_p_all, 1), stk(st_s_all, 0), stk(st_s_all, 1),
            stk(st_p_all, 2), stk(st_p_all, 3), stk(st_s_all, 2), stk(st_s_all, 3),
            stk(st_p_all, 4), stk(st_s_all, 4),
            stk(st_p_all, 5), stk(st_s_all, 5),
            stk(st_p_all, 6), stk(st_p_all, 7), stk(st_s_all, 6), stk(st_s_all, 7))
```
